```python
import jax, jax.numpy as jnp
from jax import lax
import numpy as np

D_MODEL = 2048
BATCH = 4
SEQ = 4096
DEPTH = 2

MIX_WIDTH = D_MODEL
RET_HEADS = 4
RET_QK_DIM = D_MODEL // 8
RET_V_DIM = D_MODEL // 8
RET_WIDTH = RET_HEADS * RET_V_DIM
RET_CHUNK = 256
MOBA_HEADS = 8
MOBA_HEAD_DIM = D_MODEL // 16
MOBA_WIDTH = MOBA_HEADS * MOBA_HEAD_DIM
MOBA_BLOCK = 256
MOBA_TOPK = 3
MOBA_QUERY_CHUNK = 16
D_FF = 11 * D_MODEL // 4
CONV_WIDTH = 3
NORM_EPS = 1e-6
NEG_INF = -1e30
IN_SIZES = (RET_HEADS * RET_QK_DIM, RET_HEADS * RET_QK_DIM, RET_WIDTH, RET_WIDTH,
            MOBA_WIDTH, MOBA_WIDTH, MOBA_WIDTH)
IN_COLS = sum(IN_SIZES)
IN_SPLITS = tuple(int(v) for v in np.cumsum(IN_SIZES)[:-1])

kernel_name = "hybrid_retention_moba_convffn"


def rms_norm(x, g):
    xf = x.astype(jnp.float32)
    y = xf * lax.rsqrt(jnp.mean(xf * xf, axis=-1, keepdims=True) + NORM_EPS)
    return (y * g.astype(jnp.float32)).astype(x.dtype)


def pad_to_multiple(a, axis, mult):
    pad = (-a.shape[axis]) % mult
    if pad == 0:
        return a
    widths = [(0, 0)] * a.ndim
    widths[axis] = (0, pad)
    return jnp.pad(a, widths)


def retention_log_decay():
    gamma = 1.0 - 2.0 ** (-5.0 - jnp.arange(RET_HEADS, dtype=jnp.float32))
    return jnp.log(gamma)


def chunkwise_retention(q, k, v):
    B, H, S, dk = q.shape
    dv = v.shape[-1]
    C = RET_CHUNK
    N = S // C
    dt = q.dtype
    log_g = retention_log_decay()
    i = jnp.arange(C, dtype=jnp.float32)
    diff = i[:, None] - i[None, :]
    inner_decay = jnp.where(diff >= 0, jnp.exp(log_g[:, None, None] * jnp.maximum(diff, 0.0)), 0.0)
    q_decay = jnp.exp(log_g[:, None] * (i + 1.0))
    k_decay = jnp.exp(log_g[:, None] * (C - 1.0 - i))
    chunk_decay = jnp.exp(log_g * C)
    qc = q.reshape(B, H, N, C, dk)
    kc = k.reshape(B, H, N, C, dk) * (dk ** -0.5)
    vc = v.reshape(B, H, N, C, dv)
    scores = jnp.einsum('bhncd,bhnsd->bhncs', qc, kc) * inner_decay[:, None].astype(dt)
    inner = jnp.einsum('bhncs,bhnse->bhnce', scores, vc)
    chunk_kv = jnp.einsum('bhnsd,bhnse->nbhde', kc * k_decay[:, None, :, None].astype(dt), vc)
    cdec = chunk_decay[:, None, None].astype(dt)

    def step(state, kv):
        return state * cdec + kv, state

    _, states = lax.scan(step, jnp.zeros((B, H, dk, dv), dt), chunk_kv)
    cross = jnp.einsum('bhncd,nbhde->bhnce', qc, states) * q_decay[:, None, :, None].astype(dt)
    return (inner + cross).reshape(B, H, S, dv)


def alibi_slopes(n):
    return 2.0 ** (-8.0 * (jnp.arange(n, dtype=jnp.float32) + 1.0) / n)


def moba_attention(q, k, v):
    B, H, S, dh = q.shape
    L = MOBA_BLOCK
    NB = S // L
    Qc = MOBA_QUERY_CHUNK
    NC = S // Qc
    top_k = min(MOBA_TOPK, NB)
    scale = dh ** -0.5
    kb = k.reshape(B, H, NB, L, dh)
    vb = v.reshape(B, H, NB, L, dh)
    k_mean = jnp.mean(kb.astype(jnp.float32), axis=3)
    gate = jnp.einsum('bhsd,bhnd->bhsn', q.astype(jnp.float32), k_mean)
    own_block = jnp.arange(S) // L
    fully_past = jnp.arange(NB)[None, :] < own_block[:, None]
    gate = jnp.where(fully_past, gate, -jnp.inf)
    _, sel = lax.top_k(gate, top_k)
    slopes = alibi_slopes(H)
    qs = jnp.moveaxis(q.reshape(B, H, NC, Qc, dh), 2, 0)
    sels = jnp.moveaxis(sel.reshape(B, H, NC, Qc, top_k), 2, 0)
    b_ix = jnp.arange(B)[:, None, None, None]
    h_ix = jnp.arange(H)[None, :, None, None]
    key_off = jnp.arange(L)

    def attend(args):
        c, q_c, sel_c = args
        t = c * Qc + jnp.arange(Qc)
        blk = (c * Qc) // L
        k_own = lax.dynamic_index_in_dim(kb, blk, axis=2, keepdims=False)
        v_own = lax.dynamic_index_in_dim(vb, blk, axis=2, keepdims=False)
        k_sel = kb[b_ix, h_ix, sel_c]
        v_sel = vb[b_ix, h_ix, sel_c]
        dist_own = (t[:, None] - (blk * L + key_off)[None, :]).astype(jnp.float32)
        dist_sel = (t[:, None, None] - (sel_c[..., None] * L + key_off)).astype(jnp.float32)
        logit_own = jnp.einsum('bhqd,bhld->bhql', q_c, k_own).astype(jnp.float32) * scale
        logit_sel = jnp.einsum('bhqd,bhqkld->bhqkl', q_c, k_sel).astype(jnp.float32) * scale
        logit_own = jnp.where(dist_own >= 0, logit_own - slopes[:, None, None] * dist_own, NEG_INF)
        valid = (sel_c < blk)[..., None]
        logit_sel = jnp.where(valid, logit_sel - slopes[:, None, None, None] * dist_sel, NEG_INF)
        logits = jnp.concatenate([logit_own, logit_sel.reshape(B, H, Qc, top_k * L)], axis=-1)
        p = jax.nn.softmax(logits, axis=-1).astype(v.dtype)
        out = (jnp.einsum('bhql,bhld->bhqd', p[..., :L], v_own)
               + jnp.einsum('bhqkl,bhqkld->bhqd', p[..., L:].reshape(B, H, Qc, top_k, L), v_sel))
        return out

    out = lax.map(attend, (jnp.arange(NC), qs, sels))
    return jnp.moveaxis(out, 0, 2).reshape(B, H, S, dh)


def to_heads(a, n_heads):
    B, S, _ = a.shape
    return a.reshape(B, S, n_heads, -1).transpose(0, 2, 1, 3)


def hybrid_mixer(h, w_in, ret_norm_g, q_norm_g, k_norm_g, w_out):
    B, S, _ = h.shape
    proj = h @ w_in
    rq, rk, rv, rg, mq, mk, mv = jnp.split(proj, IN_SPLITS, axis=-1)
    y_r = chunkwise_retention(pad_to_multiple(to_heads(rq, RET_HEADS), 2, RET_CHUNK),
                              pad_to_multiple(to_heads(rk, RET_HEADS), 2, RET_CHUNK),
                              pad_to_multiple(to_heads(rv, RET_HEADS), 2, RET_CHUNK))[:, :, :S]
    y_r = rms_norm(y_r.transpose(0, 2, 1, 3), ret_norm_g)
    y_r = y_r.reshape(B, S, RET_WIDTH) * jax.nn.silu(rg)
    qm = rms_norm(to_heads(mq, MOBA_HEADS), q_norm_g)
    km = rms_norm(to_heads(mk, MOBA_HEADS), k_norm_g)
    vm = to_heads(mv, MOBA_HEADS)
    y_m = moba_attention(pad_to_multiple(qm, 2, MOBA_BLOCK),
                         pad_to_multiple(km, 2, MOBA_BLOCK),
                         pad_to_multiple(vm, 2, MOBA_BLOCK))[:, :, :S]
    y_m = y_m.transpose(0, 2, 1, 3).reshape(B, S, MOBA_WIDTH)
    return jnp.concatenate([y_r, y_m], axis=-1) @ w_out


def conv_ffn(h, w_gate, w_up, conv_w, conv_b, w_down):
    S = h.shape[1]
    a = h @ w_gate
    a_pad = jnp.pad(a, ((0, 0), (CONV_WIDTH - 1, 0), (0, 0)))
    conv = conv_b
    for j in range(CONV_WIDTH):
        conv = conv + conv_w[j] * a_pad[:, j:j + S]
    return (jax.nn.silu(conv) * (h @ w_up)) @ w_down


def setup_inputs(seed: int = 0) -> dict:
    key = jax.random.key(seed)
    ks = jax.random.split(key, 13)
    nrm = jax.random.normal
    f32 = jnp.float32
    return {
        "x": nrm(ks[0], (BATCH, SEQ, D_MODEL), f32),
        "ln1_g": 1.0 + 0.02 * nrm(ks[1], (DEPTH, D_MODEL), f32),
        "w_in": nrm(ks[2], (DEPTH, D_MODEL, IN_COLS), f32) * D_MODEL ** -0.5,
        "ret_norm_g": 1.0 + 0.02 * nrm(ks[3], (DEPTH, RET_HEADS, RET_V_DIM), f32),
        "q_norm_g": 1.0 + 0.02 * nrm(ks[4], (DEPTH, MOBA_HEAD_DIM), f32),
        "k_norm_g": 1.0 + 0.02 * nrm(ks[5], (DEPTH, MOBA_HEAD_DIM), f32),
        "w_out": nrm(ks[6], (DEPTH, MIX_WIDTH, D_MODEL), f32) * MIX_WIDTH ** -0.5,
        "ln2_g": 1.0 + 0.02 * nrm(ks[7], (DEPTH, D_MODEL), f32),
        "w_gate": nrm(ks[8], (DEPTH, D_MODEL, D_FF), f32) * D_MODEL ** -0.5,
        "w_up": nrm(ks[9], (DEPTH, D_MODEL, D_FF), f32) * D_MODEL ** -0.5,
        "conv_w": nrm(ks[10], (DEPTH, CONV_WIDTH, D_FF), f32) * CONV_WIDTH ** -0.5,
        "conv_b": 0.02 * nrm(ks[11], (DEPTH, D_FF), f32),
        "w_down": nrm(ks[12], (DEPTH, D_FF, D_MODEL), f32) * D_FF ** -0.5,
    }


def reference(x, ln1_g, w_in, ret_norm_g, q_norm_g, k_norm_g, w_out, ln2_g, w_gate, w_up, conv_w, conv_b, w_down):
    for l in range(DEPTH):
        h = rms_norm(x, ln1_g[l])
        x = x + hybrid_mixer(h, w_in[l], ret_norm_g[l], q_norm_g[l], k_norm_g[l], w_out[l])
        h = rms_norm(x, ln2_g[l])
        x = x + conv_ffn(h, w_gate[l], w_up[l], conv_w[l], conv_b[l], w_down[l])
    return x
```

```python
import functools

import jax
import jax.numpy as jnp
from jax import lax
from jax.experimental import pallas as pl
from jax.experimental.pallas import tpu as pltpu

RET_HEADS = 4
RET_DIM = 256
RET_CHUNK = 256
MOBA_HEADS = 8
MOBA_DIM = 128
MOBA_BLOCK = 256
MOBA_TOPK = 3
CONV_WIDTH = 3
NORM_EPS = 1e-6
NEG_INF = -1e30
GROUP = 1024

V7X_VMEM_BYTES = 64 * 1024 * 1024
V7X_BF16_SUBLANES = 16

F32 = jnp.float32
BF16 = jnp.bfloat16


def _tiles():
    return dict(
        in_tm=512, in_tn=1024,
        out_tm=512, out_tn=1024,
        ffn_tm=512, ffn_tf=512,
        vmem_limit=int(V7X_VMEM_BYTES * 0.85),
    )


def _params(semantics, vmem_limit):
    return pltpu.CompilerParams(dimension_semantics=semantics, vmem_limit_bytes=vmem_limit)


_GATE_GROUP = 3
_MQ_GROUP = 4
_MK_GROUP = 5


def _in_proj_kernel(x_ref, g_ref, w_ref, qg_ref, kg_ref, o_ref, gate_ref, h_ref, *, tn):
    j = pl.program_id(1)
    per_group = GROUP // tn
    group = j // per_group
    sub = j % per_group

    @pl.when(j == 0)
    def _():
        x = x_ref[...]
        ms = jnp.mean(x * x, axis=-1, keepdims=True)
        h_ref[...] = (x * lax.rsqrt(ms + NORM_EPS) * g_ref[...]).astype(BF16)

    acc = jnp.dot(h_ref[...], w_ref[...], preferred_element_type=F32)

    is_gate = group == _GATE_GROUP
    is_normed = jnp.logical_or(group == _MQ_GROUP, group == _MK_GROUP)

    @pl.when(is_gate)
    def _():
        gate_ref[:, pl.ds(pl.multiple_of(sub * tn, tn), tn)] = acc

    @pl.when(jnp.logical_and(jnp.logical_not(is_gate), jnp.logical_not(is_normed)))
    def _():
        o_ref[...] = acc.astype(BF16)

    @pl.when(is_normed)
    def _():
        gain = jnp.where(group == _MQ_GROUP, qg_ref[...], kg_ref[...])
        for hh in range(tn // MOBA_DIM):
            seg = acc[:, hh * MOBA_DIM:(hh + 1) * MOBA_DIM]
            ms = jnp.mean(seg * seg, axis=-1, keepdims=True)
            o_ref[:, hh * MOBA_DIM:(hh + 1) * MOBA_DIM] = (seg * lax.rsqrt(ms + NORM_EPS) * gain).astype(BF16)


def _in_proj(x2, ln_g, w_in, q_g, k_g, t):
    T, D = x2.shape
    n_cols = w_in.shape[1]
    tm, tn = t["in_tm"], t["in_tn"]
    per_group = GROUP // tn
    gate_lo = _GATE_GROUP * per_group

    def o_map(i, j):
        return (i, jnp.where(j >= gate_lo + per_group, j - per_group, jnp.minimum(j, gate_lo - 1)))

    return pl.pallas_call(
        functools.partial(_in_proj_kernel, tn=tn),
        grid=(T // tm, n_cols // tn),
        in_specs=[
            pl.BlockSpec((tm, D), lambda i, j: (i, 0)),
            pl.BlockSpec((1, D), lambda i, j: (0, 0)),
            pl.BlockSpec((D, tn), lambda i, j: (0, j)),
            pl.BlockSpec((1, MOBA_DIM), lambda i, j: (0, 0)),
            pl.BlockSpec((1, MOBA_DIM), lambda i, j: (0, 0)),
        ],
        out_specs=[
            pl.BlockSpec((tm, tn), o_map),
            pl.BlockSpec((tm, GROUP), lambda i, j: (i, 0)),
        ],
        out_shape=[
            jax.ShapeDtypeStruct((T, n_cols - GROUP), BF16),
            jax.ShapeDtypeStruct((T, GROUP), F32),
        ],
        scratch_shapes=[pltpu.VMEM((tm, D), BF16)],
        compiler_params=_params(("arbitrary", "arbitrary"), t["vmem_limit"]),
    )(x2, ln_g.reshape(1, D), w_in, q_g.reshape(1, MOBA_DIM), k_g.reshape(1, MOBA_DIM))


def _retention_kernel(lg_ref, q_ref, k_ref, v_ref, gate_ref, g_ref, o_ref, state_ref, *, n_chunks):
    hd = pl.program_id(1)
    lg = lg_ref[hd]
    C = RET_CHUNK
    k_scale = RET_DIM ** -0.5
    row = lax.broadcasted_iota(jnp.int32, (C, C), 0)
    col = lax.broadcasted_iota(jnp.int32, (C, C), 1)
    diff = (row - col).astype(F32)
    inner_decay = jnp.where(diff >= 0, jnp.exp(lg * jnp.maximum(diff, 0.0)), 0.0) * k_scale
    pos = lax.broadcasted_iota(jnp.int32, (C, 1), 0).astype(F32)
    q_decay = jnp.exp(lg * (pos + 1.0))
    k_decay = jnp.exp(lg * (C - 1.0 - pos)) * k_scale
    chunk_decay = jnp.exp(jnp.full((1, 1), lg * C, F32))
    gain = g_ref[pl.ds(hd, 1), :]
    state_ref[...] = jnp.zeros_like(state_ref)

    def chunk(n, carry):
        sl = pl.ds(pl.multiple_of(n * C, C), C)
        q = q_ref[sl, :]
        k = k_ref[sl, :]
        v = v_ref[sl, :]
        scores = lax.dot_general(q, k, (((1,), (1,)), ((), ())), preferred_element_type=F32) * inner_decay
        inner = jnp.dot(scores.astype(BF16), v, preferred_element_type=F32)
        state = state_ref[...]
        cross = jnp.dot(q, state.astype(BF16), preferred_element_type=F32) * q_decay
        kd = (k.astype(F32) * k_decay).astype(BF16)
        kv = lax.dot_general(kd, v, (((0,), (0,)), ((), ())), preferred_element_type=F32)
        state_ref[...] = state * chunk_decay + kv
        y = inner + cross
        ms = jnp.mean(y * y, axis=-1, keepdims=True)
        y = y * lax.rsqrt(ms + NORM_EPS) * gain
        gt = gate_ref[sl, :]
        o_ref[sl, :] = (y * (gt * jax.nn.sigmoid(gt))).astype(BF16)
        return carry

    lax.fori_loop(0, n_chunks, chunk, 0)


def _retention(proj, gate, ret_g, log_decay, B, S, t):
    T = B * S
    q_off, k_off, v_off = 0, GROUP // RET_DIM, 2 * GROUP // RET_DIM
    blk = lambda off: pl.BlockSpec((S, RET_DIM), lambda b, h: (b, off + h))
    return pl.pallas_call(
        functools.partial(_retention_kernel, n_chunks=S // RET_CHUNK),
        grid=(B, RET_HEADS),
        in_specs=[
            pl.BlockSpec(memory_space=pltpu.SMEM),
            blk(q_off), blk(k_off), blk(v_off),
            pl.BlockSpec((S, RET_DIM), lambda b, h: (b, h)),
            pl.BlockSpec((RET_HEADS, RET_DIM), lambda b, h: (0, 0)),
        ],
        out_specs=pl.BlockSpec((S, RET_DIM), lambda b, h: (b, h)),
        out_shape=jax.ShapeDtypeStruct((T, RET_HEADS * RET_DIM), BF16),
        scratch_shapes=[pltpu.VMEM((RET_DIM, RET_DIM), F32)],
        compiler_params=_params(("arbitrary", "arbitrary"), t["vmem_limit"]),
    )(log_decay, proj, proj, proj, gate, ret_g)


def _moba_kernel(slope_ref, q_ref, k_ref, v_ref, o_ref, kmh_ref, kml_ref, vt_ref, sel_ref, *, n_blocks):
    hd = pl.program_id(1)
    i = pl.program_id(2)
    L = MOBA_BLOCK
    slope = slope_ref[hd]
    scale = MOBA_DIM ** -0.5
    nt = (((1,), (1,)), ((), ()))

    @pl.when(i == 0)
    def _():
        for n in range(n_blocks):
            kb = k_ref[n * L:(n + 1) * L, :].astype(F32)
            mean = jnp.sum(kb, axis=0, keepdims=True) * (1.0 / L)
            hi = mean.astype(BF16)
            kmh_ref[n:n + 1, :] = hi
            kml_ref[n:n + 1, :] = (mean - hi.astype(F32)).astype(BF16)
            vt_ref[:, n * L:(n + 1) * L] = v_ref[n * L:(n + 1) * L, :].astype(F32).T.astype(BF16)

    q = q_ref[...]

    gate = (lax.dot_general(kmh_ref[...], q, nt, preferred_element_type=F32)
            + lax.dot_general(kml_ref[...], q, nt, preferred_element_type=F32))
    blk_id = lax.broadcasted_iota(jnp.int32, (n_blocks, L), 0)
    past = blk_id < i
    gate = jnp.where(past, gate, -jnp.inf)
    rank = jnp.zeros((n_blocks, L), jnp.int32)
    for m in range(n_blocks):
        gm = gate[m:m + 1, :]
        ahead = jnp.logical_or(gm > gate, jnp.logical_and(gm == gate, blk_id > m))
        rank = rank + ahead.astype(jnp.int32)
    sel_ref[...] = jnp.logical_and(rank < MOBA_TOPK, past).astype(F32)

    key_pos = lax.broadcasted_iota(jnp.int32, (L, L), 0)
    qry_pos = lax.broadcasted_iota(jnp.int32, (L, L), 1)
    dist0 = (qry_pos - key_pos).astype(F32)
    bias0 = -slope * dist0

    base = pl.multiple_of(i * L, L)
    k_own = k_ref[pl.ds(base, L), :]
    s = lax.dot_general(k_own, q, nt, preferred_element_type=F32) * scale + bias0
    s = jnp.where(dist0 >= 0, s, NEG_INF)
    m0 = jnp.max(s, axis=0, keepdims=True)
    p = jnp.exp(s - m0)
    l0 = jnp.sum(p, axis=0, keepdims=True)
    acc0 = jnp.dot(vt_ref[:, pl.ds(base, L)], p.astype(BF16), preferred_element_type=F32)

    def past_block(j, carry):
        m_prev, l_prev, acc = carry
        off = pl.multiple_of(j * L, L)
        k_j = k_ref[pl.ds(off, L), :]
        shift = slope * ((i - j) * L).astype(F32)
        s = lax.dot_general(k_j, q, nt, preferred_element_type=F32) * scale + (bias0 - shift)
        chosen = sel_ref[pl.ds(j, 1), :] > 0.5
        s = jnp.where(chosen, s, NEG_INF)
        m_new = jnp.maximum(m_prev, jnp.max(s, axis=0, keepdims=True))
        alpha = jnp.exp(m_prev - m_new)
        p = jnp.exp(s - m_new)
        l_new = alpha * l_prev + jnp.sum(p, axis=0, keepdims=True)
        pv = jnp.dot(vt_ref[:, pl.ds(off, L)], p.astype(BF16), preferred_element_type=F32)
        return m_new, l_new, alpha * acc + pv

    _, l_fin, acc = lax.fori_loop(0, i, past_block, (m0, l0, acc0))
    o_ref[...] = (acc / l_fin).T.astype(BF16)


def _moba(proj, slopes, B, S, t):
    T = B * S
    nb = S // MOBA_BLOCK
    per = GROUP // MOBA_DIM
    q_off, k_off, v_off = 3 * per, 4 * per, 5 * per
    return pl.pallas_call(
        functools.partial(_moba_kernel, n_blocks=nb),
        grid=(B, MOBA_HEADS, nb),
        in_specs=[
            pl.BlockSpec(memory_space=pltpu.SMEM),
            pl.BlockSpec((MOBA_BLOCK, MOBA_DIM), lambda b, h, i: (b * nb + i, q_off + h)),
            pl.BlockSpec((S, MOBA_DIM), lambda b, h, i: (b, k_off + h)),
            pl.BlockSpec((S, MOBA_DIM), lambda b, h, i: (b, v_off + h)),
        ],
        out_specs=pl.BlockSpec((MOBA_BLOCK, MOBA_DIM), lambda b, h, i: (b * nb + i, h)),
        out_shape=jax.ShapeDtypeStruct((T, MOBA_HEADS * MOBA_DIM), BF16),
        scratch_shapes=[
            pltpu.VMEM((nb, MOBA_DIM), BF16),
            pltpu.VMEM((nb, MOBA_DIM), BF16),
            pltpu.VMEM((MOBA_DIM, S), BF16),
            pltpu.VMEM((nb, MOBA_BLOCK), F32),
        ],
        compiler_params=_params(("arbitrary", "arbitrary", "arbitrary"), t["vmem_limit"]),
    )(slopes, proj, proj, proj)


def _out_proj_kernel(yr_ref, ym_ref, wr_ref, wm_ref, x_ref, o_ref):
    o_ref[...] = (x_ref[...]
                  + jnp.dot(yr_ref[...], wr_ref[...], preferred_element_type=F32)
                  + jnp.dot(ym_ref[...], wm_ref[...], preferred_element_type=F32))


def _out_proj(y_r, y_m, w_out, x2, t):
    T, D = x2.shape
    tm, tn = t["out_tm"], t["out_tn"]
    kr, km = y_r.shape[1], y_m.shape[1]
    return pl.pallas_call(
        _out_proj_kernel,
        grid=(T // tm, D // tn),
        in_specs=[
            pl.BlockSpec((tm, kr), lambda i, j: (i, 0)),
            pl.BlockSpec((tm, km), lambda i, j: (i, 0)),
            pl.BlockSpec((kr, tn), lambda i, j: (0, j)),
            pl.BlockSpec((km, tn), lambda i, j: (kr // km, j)),
            pl.BlockSpec((tm, tn), lambda i, j: (i, j)),
        ],
        out_specs=pl.BlockSpec((tm, tn), lambda i, j: (i, j)),
        out_shape=jax.ShapeDtypeStruct((T, D), F32),
        compiler_params=_params(("arbitrary", "arbitrary"), t["vmem_limit"]),
    )(y_r, y_m, w_out, w_out, x2)


HALO = V7X_BF16_SUBLANES


def _ffn_kernel(x_ref, xh_ref, g_ref, wg_ref, wu_ref, cw_ref, cb_ref, wd_ref, o_ref, h_ref, *, tm, tiles_per_seq):
    i = pl.program_id(0)
    f = pl.program_id(1)

    def norm(x):
        ms = jnp.mean(x * x, axis=-1, keepdims=True)
        return (x * lax.rsqrt(ms + NORM_EPS) * g_ref[...]).astype(BF16)

    @pl.when(f == 0)
    def _():
        x = x_ref[...]
        h_ref[HALO:, :] = norm(x)
        seq_start = (i % tiles_per_seq) == 0
        h_ref[:HALO, :] = jnp.where(seq_start, jnp.zeros((HALO, x.shape[1]), BF16), norm(xh_ref[...]))
        o_ref[...] = x

    a = jnp.dot(h_ref[...], wg_ref[...], preferred_element_type=F32)
    u = jnp.dot(h_ref[HALO:, :], wu_ref[...], preferred_element_type=F32)
    conv = cb_ref[...]
    for tap in range(CONV_WIDTH):
        lo = HALO - (CONV_WIDTH - 1) + tap
        conv = conv + cw_ref[tap:tap + 1, :] * a[lo:lo + tm, :]
    act = (conv * jax.nn.sigmoid(conv) * u).astype(BF16)
    o_ref[...] += jnp.dot(act, wd_ref[...], preferred_element_type=F32)


def _conv_ffn(x2, ln_g, w_gate, w_up, conv_w, conv_b, w_down, S, t):
    T, D = x2.shape
    F = w_gate.shape[1]
    tm, tf = t["ffn_tm"], t["ffn_tf"]
    halo_blocks = tm // HALO
    return pl.pallas_call(
        functools.partial(_ffn_kernel, tm=tm, tiles_per_seq=S // tm),
        grid=(T // tm, F // tf),
        in_specs=[
            pl.BlockSpec((tm, D), lambda i, f: (i, 0)),
            pl.BlockSpec((HALO, D), lambda i, f: (jnp.maximum(i * halo_blocks - 1, 0), 0)),
            pl.BlockSpec((1, D), lambda i, f: (0, 0)),
            pl.BlockSpec((D, tf), lambda i, f: (0, f)),
            pl.BlockSpec((D, tf), lambda i, f: (0, f)),
            pl.BlockSpec((CONV_WIDTH, tf), lambda i, f: (0, f)),
            pl.BlockSpec((1, tf), lambda i, f: (0, f)),
            pl.BlockSpec((tf, D), lambda i, f: (f, 0)),
        ],
        out_specs=pl.BlockSpec((tm, D), lambda i, f: (i, 0)),
        out_shape=jax.ShapeDtypeStruct((T, D), F32),
        scratch_shapes=[pltpu.VMEM((HALO + tm, D), BF16)],
        compiler_params=_params(("arbitrary", "arbitrary"), t["vmem_limit"]),
    )(x2, x2, ln_g.reshape(1, D), w_gate, w_up, conv_w, conv_b.reshape(1, F), w_down)


def kernel(x, ln1_g, w_in, ret_norm_g, q_norm_g, k_norm_g, w_out, ln2_g, w_gate, w_up, conv_w, conv_b, w_down):
    B, S, D = x.shape
    depth = w_in.shape[0]
    t = _tiles()
    assert S % RET_CHUNK == 0 and S % MOBA_BLOCK == 0 and S % t["ffn_tm"] == 0
    assert w_in.shape[2] == 7 * GROUP

    log_decay = jnp.log(1.0 - 2.0 ** (-5.0 - jnp.arange(RET_HEADS, dtype=F32)))
    slopes = 2.0 ** (-8.0 * (jnp.arange(MOBA_HEADS, dtype=F32) + 1.0) / MOBA_HEADS)

    x2 = x.reshape(B * S, D)
    for l in range(depth):
        proj, gate = _in_proj(x2, ln1_g[l], w_in[l].astype(BF16), q_norm_g[l], k_norm_g[l], t)
        y_r = _retention(proj, gate, ret_norm_g[l], log_decay, B, S, t)
        y_m = _moba(proj, slopes, B, S, t)
        x2 = _out_proj(y_r, y_m, w_out[l].astype(BF16), x2, t)
        x2 = _conv_ffn(x2, ln2_g[l], w_gate[l].astype(BF16), w_up[l].astype(BF16),
                       conv_w[l], conv_b[l], w_down[l].astype(BF16), S, t)
    return x2.reshape(B, S, D)
```

```python
import functools

import jax
import jax.numpy as jnp
from jax import lax
from jax.experimental import pallas as pl
from jax.experimental.pallas import tpu as pltpu

RET_HEADS = 4
RET_DIM = 256
RET_CHUNK = 256
MOBA_HEADS = 8
MOBA_DIM = 128
MOBA_BLOCK = 256
MOBA_TOPK = 3
CONV_WIDTH = 3
NORM_EPS = 1e-6
NEG_INF = -1e30
GROUP = 1024

V7X_VMEM_BYTES = 64 * 1024 * 1024
V7X_BF16_SUBLANES = 16

F32 = jnp.float32
BF16 = jnp.bfloat16


def _tiles():
    return dict(
        in_tm=1024, in_tn=1024,
        out_tm=1024, out_tn=1024,
        ffn_tm=1024, ffn_tf=512,
        moba_heads_per_step=4,
        vmem_limit=int(V7X_VMEM_BYTES * 0.92),
    )


def _params(semantics, vmem_limit):
    return pltpu.CompilerParams(dimension_semantics=semantics, vmem_limit_bytes=vmem_limit)


_GATE_GROUP = 3
_MQ_GROUP = 4
_MK_GROUP = 5


def _in_proj_kernel(x_ref, g_ref, w_ref, qg_ref, kg_ref, o_ref, gate_ref, h_ref, *, tn):
    j = pl.program_id(1)
    per_group = GROUP // tn
    group = j // per_group
    sub = j % per_group

    @pl.when(j == 0)
    def _():
        x = x_ref[...]
        ms = jnp.mean(x * x, axis=-1, keepdims=True)
        h_ref[...] = (x * lax.rsqrt(ms + NORM_EPS) * g_ref[...]).astype(BF16)

    acc = jnp.dot(h_ref[...], w_ref[...], preferred_element_type=F32)

    is_gate = group == _GATE_GROUP
    is_normed = jnp.logical_or(group == _MQ_GROUP, group == _MK_GROUP)

    @pl.when(is_gate)
    def _():
        gate_ref[:, pl.ds(pl.multiple_of(sub * tn, tn), tn)] = acc

    @pl.when(jnp.logical_and(jnp.logical_not(is_gate), jnp.logical_not(is_normed)))
    def _():
        o_ref[...] = acc.astype(BF16)

    @pl.when(is_normed)
    def _():
        gain = jnp.where(group == _MQ_GROUP, qg_ref[...], kg_ref[...])
        for hh in range(tn // MOBA_DIM):
            seg = acc[:, hh * MOBA_DIM:(hh + 1) * MOBA_DIM]
            ms = jnp.mean(seg * seg, axis=-1, keepdims=True)
            o_ref[:, hh * MOBA_DIM:(hh + 1) * MOBA_DIM] = (seg * lax.rsqrt(ms + NORM_EPS) * gain).astype(BF16)


def _in_proj(x2, ln_g, w_in, q_g, k_g, t):
    T, D = x2.shape
    n_cols = w_in.shape[1]
    tm, tn = t["in_tm"], t["in_tn"]
    per_group = GROUP // tn
    gate_lo = _GATE_GROUP * per_group

    def o_map(i, j):
        return (i, jnp.where(j >= gate_lo + per_group, j - per_group, jnp.minimum(j, gate_lo - 1)))

    return pl.pallas_call(
        functools.partial(_in_proj_kernel, tn=tn),
        grid=(T // tm, n_cols // tn),
        in_specs=[
            pl.BlockSpec((tm, D), lambda i, j: (i, 0), pipeline_mode=pl.Buffered(1)),
            pl.BlockSpec((1, D), lambda i, j: (0, 0)),
            pl.BlockSpec((D, tn), lambda i, j: (0, j)),
            pl.BlockSpec((1, MOBA_DIM), lambda i, j: (0, 0)),
            pl.BlockSpec((1, MOBA_DIM), lambda i, j: (0, 0)),
        ],
        out_specs=[
            pl.BlockSpec((tm, tn), o_map),
            pl.BlockSpec((tm, GROUP), lambda i, j: (i, 0)),
        ],
        out_shape=[
            jax.ShapeDtypeStruct((T, n_cols - GROUP), BF16),
            jax.ShapeDtypeStruct((T, GROUP), F32),
        ],
        scratch_shapes=[pltpu.VMEM((tm, D), BF16)],
        compiler_params=_params(("arbitrary", "arbitrary"), t["vmem_limit"]),
    )(x2, ln_g.reshape(1, D), w_in, q_g.reshape(1, MOBA_DIM), k_g.reshape(1, MOBA_DIM))


def _retention_kernel(lg_ref, q_ref, k_ref, v_ref, gate_ref, g_ref, o_ref, state_ref, *, n_chunks):
    hd = pl.program_id(1)
    lg = lg_ref[hd]
    C = RET_CHUNK
    k_scale = RET_DIM ** -0.5
    row = lax.broadcasted_iota(jnp.int32, (C, C), 0)
    col = lax.broadcasted_iota(jnp.int32, (C, C), 1)
    diff = (row - col).astype(F32)
    inner_decay = jnp.where(diff >= 0, jnp.exp(lg * jnp.maximum(diff, 0.0)), 0.0) * k_scale
    pos = lax.broadcasted_iota(jnp.int32, (C, 1), 0).astype(F32)
    q_decay = jnp.exp(lg * (pos + 1.0))
    k_decay = jnp.exp(lg * (C - 1.0 - pos)) * k_scale
    chunk_decay = jnp.exp(jnp.full((1, 1), lg * C, F32))
    gain = g_ref[pl.ds(hd, 1), :]
    state_ref[...] = jnp.zeros_like(state_ref)

    def chunk(n, carry):
        sl = pl.ds(pl.multiple_of(n * C, C), C)
        q = q_ref[sl, :]
        k = k_ref[sl, :]
        v = v_ref[sl, :]
        scores = lax.dot_general(q, k, (((1,), (1,)), ((), ())), preferred_element_type=F32) * inner_decay
        inner = jnp.dot(scores.astype(BF16), v, preferred_element_type=F32)
        state = state_ref[...]
        cross = jnp.dot(q, state.astype(BF16), preferred_element_type=F32) * q_decay
        kd = (k.astype(F32) * k_decay).astype(BF16)
        kv = lax.dot_general(kd, v, (((0,), (0,)), ((), ())), preferred_element_type=F32)
        state_ref[...] = state * chunk_decay + kv
        y = inner + cross
        ms = jnp.mean(y * y, axis=-1, keepdims=True)
        y = y * lax.rsqrt(ms + NORM_EPS) * gain
        gt = gate_ref[sl, :]
        o_ref[sl, :] = (y * (gt * jax.nn.sigmoid(gt))).astype(BF16)
        return carry

    lax.fori_loop(0, n_chunks, chunk, 0)


def _retention(proj, gate, ret_g, log_decay, B, S, t):
    T = B * S
    q_off, k_off, v_off = 0, GROUP // RET_DIM, 2 * GROUP // RET_DIM
    blk = lambda off: pl.BlockSpec((S, RET_DIM), lambda b, h: (b, off + h))
    return pl.pallas_call(
        functools.partial(_retention_kernel, n_chunks=S // RET_CHUNK),
        grid=(B, RET_HEADS),
        in_specs=[
            pl.BlockSpec(memory_space=pltpu.SMEM),
            blk(q_off), blk(k_off), blk(v_off),
            pl.BlockSpec((S, RET_DIM), lambda b, h: (b, h)),
            pl.BlockSpec((RET_HEADS, RET_DIM), lambda b, h: (0, 0)),
        ],
        out_specs=pl.BlockSpec((S, RET_DIM), lambda b, h: (b, h)),
        out_shape=jax.ShapeDtypeStruct((T, RET_HEADS * RET_DIM), BF16),
        scratch_shapes=[pltpu.VMEM((RET_DIM, RET_DIM), F32)],
        compiler_params=_params(("arbitrary", "arbitrary"), t["vmem_limit"]),
    )(log_decay, proj, proj, proj, gate, ret_g)


def _moba_kernel(slope_ref, q_ref, k_ref, v_ref, o_ref, kmh_ref, kml_ref, vt_ref, sel_ref, bias_ref,
                 *, n_blocks, heads):
    hg = pl.program_id(1)
    i = pl.program_id(2)
    L = MOBA_BLOCK
    dh = MOBA_DIM
    scale = dh ** -0.5
    nt = (((1,), (1,)), ((), ()))
    lanes = lambda g: slice(g * dh, (g + 1) * dh)

    key_pos = lax.broadcasted_iota(jnp.int32, (L, L), 0)
    qry_pos = lax.broadcasted_iota(jnp.int32, (L, L), 1)
    dist0 = (qry_pos - key_pos).astype(F32)

    @pl.when(i == 0)
    def _():
        for g in range(heads):
            bias_ref[g] = -slope_ref[hg * heads + g] * dist0
            for n in range(n_blocks):
                rows = slice(n * L, (n + 1) * L)
                kb = k_ref[rows, lanes(g)].astype(F32)
                mean = jnp.sum(kb, axis=0, keepdims=True) * (1.0 / L)
                hi = mean.astype(BF16)
                kmh_ref[g, n:n + 1, :] = hi
                kml_ref[g, n:n + 1, :] = (mean - hi.astype(F32)).astype(BF16)
                vt_ref[g, :, rows] = v_ref[rows, lanes(g)].astype(F32).T.astype(BF16)

    blk_id = lax.broadcasted_iota(jnp.int32, (n_blocks, L), 0)
    past = blk_id < i
    base = pl.multiple_of(i * L, L)
    qs = [q_ref[:, lanes(g)] for g in range(heads)]
    gates = [lax.dot_general(kmh_ref[g], qs[g], nt, preferred_element_type=F32)
             + lax.dot_general(kml_ref[g], qs[g], nt, preferred_element_type=F32) for g in range(heads)]
    own = [lax.dot_general(k_ref[pl.ds(base, L), lanes(g)], qs[g], nt, preferred_element_type=F32)
           for g in range(heads)]
    init = []
    for g in range(heads):
        gate = jnp.where(past, gates[g], -jnp.inf)
        rank = jnp.zeros((n_blocks, L), jnp.int32)
        for m in range(n_blocks):
            gm = gate[m:m + 1, :]
            ahead = jnp.logical_or(gm > gate, jnp.logical_and(gm == gate, blk_id > m))
            rank = rank + ahead.astype(jnp.int32)
        sel_ref[g] = jnp.logical_and(rank < MOBA_TOPK, past).astype(F32)

        s = own[g] * scale + bias_ref[g]
        s = jnp.where(dist0 >= 0, s, NEG_INF)
        m0 = jnp.max(s, axis=0, keepdims=True)
        p = jnp.exp(s - m0)
        l0 = jnp.sum(p, axis=0, keepdims=True)
        acc0 = jnp.dot(vt_ref[g, :, pl.ds(base, L)], p.astype(BF16), preferred_element_type=F32)
        init += [m0, l0, acc0]

    def past_block(j, carry):
        off = pl.multiple_of(j * L, L)
        blocks_back = ((i - j) * L).astype(F32)
        out = []
        raw = [lax.dot_general(k_ref[pl.ds(off, L), lanes(g)], qs[g], nt, preferred_element_type=F32)
               for g in range(heads)]
        for g in range(heads):
            m_prev, l_prev, acc = carry[3 * g:3 * g + 3]
            row = jnp.where(sel_ref[g, pl.ds(j, 1), :] > 0.5, -slope_ref[hg * heads + g] * blocks_back, NEG_INF)
            s = (raw[g] * scale + bias_ref[g]) + row
            m_new = jnp.maximum(m_prev, jnp.max(s, axis=0, keepdims=True))
            alpha = jnp.exp(m_prev - m_new)
            p = jnp.exp(s - m_new)
            l_new = alpha * l_prev + jnp.sum(p, axis=0, keepdims=True)
            pv = jnp.dot(vt_ref[g, :, pl.ds(off, L)], p.astype(BF16), preferred_element_type=F32)
            out += [m_new, l_new, alpha * acc + pv]
        return tuple(out)

    fin = lax.fori_loop(0, i, past_block, tuple(init))
    for g in range(heads):
        o_ref[:, lanes(g)] = (fin[3 * g + 2] / fin[3 * g + 1]).T.astype(BF16)


def _moba(proj, slopes, B, S, t):
    T = B * S
    nb = S // MOBA_BLOCK
    G = t["moba_heads_per_step"]
    per = GROUP // (G * MOBA_DIM)
    q_off, k_off, v_off = 3 * per, 4 * per, 5 * per
    return pl.pallas_call(
        functools.partial(_moba_kernel, n_blocks=nb, heads=G),
        grid=(B, MOBA_HEADS // G, nb),
        in_specs=[
            pl.BlockSpec(memory_space=pltpu.SMEM),
            pl.BlockSpec((MOBA_BLOCK, G * MOBA_DIM), lambda b, h, i: (b * nb + i, q_off + h)),
            pl.BlockSpec((S, G * MOBA_DIM), lambda b, h, i: (b, k_off + h)),
            pl.BlockSpec((S, G * MOBA_DIM), lambda b, h, i: (b, v_off + h)),
        ],
        out_specs=pl.BlockSpec((MOBA_BLOCK, G * MOBA_DIM), lambda b, h, i: (b * nb + i, h)),
        out_shape=jax.ShapeDtypeStruct((T, MOBA_HEADS * MOBA_DIM), BF16),
        scratch_shapes=[
            pltpu.VMEM((G, nb, MOBA_DIM), BF16),
            pltpu.VMEM((G, nb, MOBA_DIM), BF16),
            pltpu.VMEM((G, MOBA_DIM, S), BF16),
            pltpu.VMEM((G, nb, MOBA_BLOCK), F32),
            pltpu.VMEM((G, MOBA_BLOCK, MOBA_BLOCK), F32),
        ],
        compiler_params=_params(("arbitrary", "arbitrary", "arbitrary"), t["vmem_limit"]),
    )(slopes, proj, proj, proj)


def _out_proj_kernel(yr_ref, ym_ref, wr_ref, wm_ref, x_ref, o_ref):
    o_ref[...] = (x_ref[...]
                  + jnp.dot(yr_ref[...], wr_ref[...], preferred_element_type=F32)
                  + jnp.dot(ym_ref[...], wm_ref[...], preferred_element_type=F32))


def _out_proj(y_r, y_m, w_out, x2, t):
    T, D = x2.shape
    tm, tn = t["out_tm"], t["out_tn"]
    kr, km = y_r.shape[1], y_m.shape[1]
    return pl.pallas_call(
        _out_proj_kernel,
        grid=(T // tm, D // tn),
        in_specs=[
            pl.BlockSpec((tm, kr), lambda i, j: (i, 0)),
            pl.BlockSpec((tm, km), lambda i, j: (i, 0)),
            pl.BlockSpec((kr, tn), lambda i, j: (0, j)),
            pl.BlockSpec((km, tn), lambda i, j: (kr // km, j)),
            pl.BlockSpec((tm, tn), lambda i, j: (i, j)),
        ],
        out_specs=pl.BlockSpec((tm, tn), lambda i, j: (i, j)),
        out_shape=jax.ShapeDtypeStruct((T, D), F32),
        compiler_params=_params(("arbitrary", "arbitrary"), t["vmem_limit"]),
    )(y_r, y_m, w_out, w_out, x2)


HALO = V7X_BF16_SUBLANES


def _ffn_kernel(x_ref, xh_ref, g_ref, wg_ref, wu_ref, cw_ref, cb_ref, wd_ref, o_ref, h_ref, *, tm, tiles_per_seq):
    i = pl.program_id(0)
    f = pl.program_id(1)

    def norm(x):
        ms = jnp.mean(x * x, axis=-1, keepdims=True)
        return (x * lax.rsqrt(ms + NORM_EPS) * g_ref[...]).astype(BF16)

    @pl.when(f == 0)
    def _():
        x = x_ref[...]
        h_ref[HALO:, :] = norm(x)
        seq_start = (i % tiles_per_seq) == 0
        h_ref[:HALO, :] = jnp.where(seq_start, jnp.zeros((HALO, x.shape[1]), BF16), norm(xh_ref[...]))
        o_ref[...] = x

    a = jnp.dot(h_ref[...], wg_ref[...], preferred_element_type=F32)
    u = jnp.dot(h_ref[HALO:, :], wu_ref[...], preferred_element_type=F32)
    conv = cb_ref[...]
    for tap in range(CONV_WIDTH):
        lo = HALO - (CONV_WIDTH - 1) + tap
        conv = conv + cw_ref[tap:tap + 1, :] * a[lo:lo + tm, :]
    act = (conv * jax.nn.sigmoid(conv) * u).astype(BF16)
    o_ref[...] += jnp.dot(act, wd_ref[...], preferred_element_type=F32)


def _conv_ffn(x2, ln_g, w_gate, w_up, conv_w, conv_b, w_down, S, t):
    T, D = x2.shape
    F = w_gate.shape[1]
    tm, tf = t["ffn_tm"], t["ffn_tf"]
    halo_blocks = tm // HALO
    return pl.pallas_call(
        functools.partial(_ffn_kernel, tm=tm, tiles_per_seq=S // tm),
        grid=(T // tm, F // tf),
        in_specs=[
            pl.BlockSpec((tm, D), lambda i, f: (i, 0), pipeline_mode=pl.Buffered(1)),
            pl.BlockSpec((HALO, D), lambda i, f: (jnp.maximum(i * halo_blocks - 1, 0), 0)),
            pl.BlockSpec((1, D), lambda i, f: (0, 0)),
            pl.BlockSpec((D, tf), lambda i, f: (0, f)),
            pl.BlockSpec((D, tf), lambda i, f: (0, f)),
            pl.BlockSpec((CONV_WIDTH, tf), lambda i, f: (0, f)),
            pl.BlockSpec((1, tf), lambda i, f: (0, f)),
            pl.BlockSpec((tf, D), lambda i, f: (f, 0)),
        ],
        out_specs=pl.BlockSpec((tm, D), lambda i, f: (i, 0)),
        out_shape=jax.ShapeDtypeStruct((T, D), F32),
        scratch_shapes=[pltpu.VMEM((HALO + tm, D), BF16)],
        compiler_params=_params(("arbitrary", "arbitrary"), t["vmem_limit"]),
    )(x2, x2, ln_g.reshape(1, D), w_gate, w_up, conv_w, conv_b.reshape(1, F), w_down)


def kernel(x, ln1_g, w_in, ret_norm_g, q_norm_g, k_norm_g, w_out, ln2_g, w_gate, w_up, conv_w, conv_b, w_down):
    B, S, D = x.shape
    depth = w_in.shape[0]
    t = _tiles()
    assert S % RET_CHUNK == 0 and S % MOBA_BLOCK == 0 and S % t["ffn_tm"] == 0
    assert w_in.shape[2] == 7 * GROUP

    log_decay = jnp.log(1.0 - 2.0 ** (-5.0 - jnp.arange(RET_HEADS, dtype=F32)))
    slopes = 2.0 ** (-8.0 * (jnp.arange(MOBA_HEADS, dtype=F32) + 1.0) / MOBA_HEADS)

    x2 = x.reshape(B * S, D)
    for l in range(depth):
        proj, gate = _in_proj(x2, ln1_g[l], w_in[l].astype(BF16), q_norm_g[l], k_norm_g[l], t)
        y_r = _retention(proj, gate, ret_norm_g[l], log_decay, B, S, t)
        y_m = _moba(proj, slopes, B, S, t)
        x2 = _out_proj(y_r, y_m, w_out[l].astype(BF16), x2, t)
        x2 = _conv_ffn(x2, ln2_g[l], w_gate[l].astype(BF16), w_up[l].astype(BF16),
                       conv_w[l], conv_b[l], w_down[l].astype(BF16), S, t)
    return x2.reshape(B, S, D)
```

```python
import functools

import jax
import jax.numpy as jnp
from jax import lax
from jax.experimental import pallas as pl
from jax.experimental.pallas import tpu as pltpu

RET_HEADS = 4
RET_DIM = 256
RET_CHUNK = 256
MOBA_HEADS = 8
MOBA_DIM = 128
MOBA_BLOCK = 256
MOBA_TOPK = 3
CONV_WIDTH = 3
NORM_EPS = 1e-6
NEG_INF = -1e30
LOG2E = 1.4426950408889634
GROUP = 1024

V7X_VMEM_BYTES = 64 * 1024 * 1024
V7X_BF16_SUBLANES = 16
V7X_MXU_COLS = 256

F32 = jnp.float32
BF16 = jnp.bfloat16


def _tiles():
    return dict(
        in_tm=1024, in_tn=1024,
        out_tm=1024, out_tn=1024,
        ffn_tm=1024, ffn_tf=512,
        moba_heads_per_step=4,
        vmem_limit=int(V7X_VMEM_BYTES * 0.92),
    )


def _params(semantics, vmem_limit):
    return pltpu.CompilerParams(dimension_semantics=semantics, vmem_limit_bytes=vmem_limit)


_GATE_GROUP = 3
_MQ_GROUP = 4
_MK_GROUP = 5


def _in_proj_kernel(x_ref, g_ref, w_ref, qg_ref, kg_ref, o_ref, gate_ref, h_ref, *, tn):
    j = pl.program_id(1)
    per_group = GROUP // tn
    group = j // per_group
    sub = j % per_group
    cw = V7X_MXU_COLS
    cols = lambda c: slice(c * cw, (c + 1) * cw)

    @pl.when(j == 0)
    def _():
        x = x_ref[...]
        ms = jnp.mean(x * x, axis=-1, keepdims=True)
        h_ref[...] = (x * lax.rsqrt(ms + NORM_EPS) * g_ref[...]).astype(BF16)

    is_gate = group == _GATE_GROUP
    is_normed = jnp.logical_or(group == _MQ_GROUP, group == _MK_GROUP)

    @pl.when(is_gate)
    def _():
        for c in range(tn // cw):
            acc = jnp.dot(h_ref[...], w_ref[:, cols(c)], preferred_element_type=F32)
            gate_ref[:, pl.ds(pl.multiple_of(sub * tn + c * cw, cw), cw)] = acc

    @pl.when(jnp.logical_and(jnp.logical_not(is_gate), jnp.logical_not(is_normed)))
    def _():
        for c in range(tn // cw):
            acc = jnp.dot(h_ref[...], w_ref[:, cols(c)], preferred_element_type=F32)
            o_ref[:, cols(c)] = acc.astype(BF16)

    @pl.when(is_normed)
    def _():
        gain = jnp.where(group == _MQ_GROUP, qg_ref[...] * (MOBA_DIM ** -0.5 * LOG2E), kg_ref[...])
        for c in range(tn // cw):
            acc = jnp.dot(h_ref[...], w_ref[:, cols(c)], preferred_element_type=F32)
            for hh in range(cw // MOBA_DIM):
                seg = acc[:, hh * MOBA_DIM:(hh + 1) * MOBA_DIM]
                ms = jnp.mean(seg * seg, axis=-1, keepdims=True)
                lo = c * cw + hh * MOBA_DIM
                o_ref[:, lo:lo + MOBA_DIM] = (seg * lax.rsqrt(ms + NORM_EPS) * gain).astype(BF16)


def _in_proj(x2, ln_g, w_in, layer, q_g, k_g, t):
    T, D = x2.shape
    n_cols = w_in.shape[2]
    tm, tn = t["in_tm"], t["in_tn"]
    per_group = GROUP // tn
    gate_lo = _GATE_GROUP * per_group

    def o_map(i, j):
        return (i, jnp.where(j >= gate_lo + per_group, j - per_group, jnp.minimum(j, gate_lo - 1)))

    return pl.pallas_call(
        functools.partial(_in_proj_kernel, tn=tn),
        grid=(T // tm, n_cols // tn),
        in_specs=[
            pl.BlockSpec((tm, D), lambda i, j: (i, 0), pipeline_mode=pl.Buffered(1)),
            pl.BlockSpec((1, D), lambda i, j: (0, 0)),
            pl.BlockSpec((None, D, tn), lambda i, j: (layer, 0, j)),
            pl.BlockSpec((1, MOBA_DIM), lambda i, j: (0, 0)),
            pl.BlockSpec((1, MOBA_DIM), lambda i, j: (0, 0)),
        ],
        out_specs=[
            pl.BlockSpec((tm, tn), o_map),
            pl.BlockSpec((tm, GROUP), lambda i, j: (i, 0)),
        ],
        out_shape=[
            jax.ShapeDtypeStruct((T, n_cols - GROUP), BF16),
            jax.ShapeDtypeStruct((T, GROUP), F32),
        ],
        scratch_shapes=[pltpu.VMEM((tm, D), BF16)],
        compiler_params=_params(("arbitrary", "arbitrary"), t["vmem_limit"]),
    )(x2, ln_g.reshape(1, D), w_in, q_g.reshape(1, MOBA_DIM), k_g.reshape(1, MOBA_DIM))


def _retention_kernel(lg_ref, q_ref, k_ref, v_ref, gate_ref, g_ref, o_ref, state_ref, *, n_chunks):
    hd = pl.program_id(1)
    lg = lg_ref[hd]
    C = RET_CHUNK
    k_scale = RET_DIM ** -0.5
    row = lax.broadcasted_iota(jnp.int32, (C, C), 0)
    col = lax.broadcasted_iota(jnp.int32, (C, C), 1)
    diff = (row - col).astype(F32)
    inner_decay = jnp.where(diff >= 0, jnp.exp(lg * jnp.maximum(diff, 0.0)), 0.0) * k_scale
    pos = lax.broadcasted_iota(jnp.int32, (C, 1), 0).astype(F32)
    q_decay = jnp.exp(lg * (pos + 1.0))
    k_decay = jnp.exp(lg * (C - 1.0 - pos)) * k_scale
    chunk_decay = jnp.exp(jnp.full((1, 1), lg * C, F32))
    gain = g_ref[pl.ds(hd, 1), :]
    state_ref[...] = jnp.zeros_like(state_ref)

    def chunk(n, carry):
        sl = pl.ds(pl.multiple_of(n * C, C), C)
        q = q_ref[sl, :]
        k = k_ref[sl, :]
        v = v_ref[sl, :]
        scores = lax.dot_general(q, k, (((1,), (1,)), ((), ())), preferred_element_type=F32) * inner_decay
        inner = jnp.dot(scores.astype(BF16), v, preferred_element_type=F32)
        state = state_ref[...]
        cross = jnp.dot(q, state.astype(BF16), preferred_element_type=F32) * q_decay
        kd = (k.astype(F32) * k_decay).astype(BF16)
        kv = lax.dot_general(kd, v, (((0,), (0,)), ((), ())), preferred_element_type=F32)
        state_ref[...] = state * chunk_decay + kv
        y = inner + cross
        ms = jnp.mean(y * y, axis=-1, keepdims=True)
        y = y * lax.rsqrt(ms + NORM_EPS) * gain
        gt = gate_ref[sl, :]
        o_ref[sl, :] = (y * (gt * jax.nn.sigmoid(gt))).astype(BF16)
        return carry

    lax.fori_loop(0, n_chunks, chunk, 0)


def _retention(proj, gate, ret_g, log_decay, B, S, t):
    T = B * S
    q_off, k_off, v_off = 0, GROUP // RET_DIM, 2 * GROUP // RET_DIM
    blk = lambda off: pl.BlockSpec((S, RET_DIM), lambda b, h: (b, off + h))
    return pl.pallas_call(
        functools.partial(_retention_kernel, n_chunks=S // RET_CHUNK),
        grid=(B, RET_HEADS),
        in_specs=[
            pl.BlockSpec(memory_space=pltpu.SMEM),
            blk(q_off), blk(k_off), blk(v_off),
            pl.BlockSpec((S, RET_DIM), lambda b, h: (b, h)),
            pl.BlockSpec((RET_HEADS, RET_DIM), lambda b, h: (0, 0)),
        ],
        out_specs=pl.BlockSpec((S, RET_DIM), lambda b, h: (b, h)),
        out_shape=jax.ShapeDtypeStruct((T, RET_HEADS * RET_DIM), BF16),
        scratch_shapes=[pltpu.VMEM((RET_DIM, RET_DIM), F32)],
        compiler_params=_params(("arbitrary", "arbitrary"), t["vmem_limit"]),
    )(log_decay, proj, proj, proj, gate, ret_g)


def _moba_kernel(slope_ref, q_ref, k_ref, v_ref, o_ref, kmh_ref, kml_ref, vt_ref, kx_ref, qa_ref,
                 *, n_blocks, heads):
    hg = pl.program_id(1)
    i = pl.program_id(2)
    L = MOBA_BLOCK
    dh = MOBA_DIM
    lanes = lambda g: slice(g * dh, (g + 1) * dh)
    CONST_ROWS = 8
    MASK_ROW0 = CONST_ROWS

    @pl.when(i == 0)
    def _():
        lane = lax.broadcasted_iota(jnp.int32, (L, dh), 1)
        key = lax.broadcasted_iota(jnp.int32, (L, dh), 0)
        for n in range(n_blocks):
            kx = jnp.where(lane < 3, key, jnp.where(lane < 6, n * L, jnp.where(lane == MASK_ROW0 + n, 1, 0)))
            kx_ref[n] = kx.astype(F32).astype(BF16)
        for g in range(heads):
            for n in range(n_blocks):
                rows = slice(n * L, (n + 1) * L)
                kb = k_ref[rows, lanes(g)].astype(F32)
                mean = jnp.sum(kb, axis=0, keepdims=True) * (1.0 / L)
                hi = mean.astype(BF16)
                kmh_ref[g, n:n + 1, :] = hi
                kml_ref[g, n:n + 1, :] = (mean - hi.astype(F32)).astype(BF16)
                vt_ref[g, :, rows] = v_ref[rows, lanes(g)].astype(F32).T.astype(BF16)

    blk_id = lax.broadcasted_iota(jnp.int32, (n_blocks, L), 0)
    past = blk_id < i
    base = pl.multiple_of(i * L, L)
    part_row = lax.broadcasted_iota(jnp.int32, (CONST_ROWS, L), 0)
    causal = (lax.broadcasted_iota(jnp.int32, (L, L), 1) >= lax.broadcasted_iota(jnp.int32, (L, L), 0))

    qts = [q_ref[:, lanes(g)].astype(F32).T.astype(BF16) for g in range(heads)]
    gates = [jnp.dot(kmh_ref[g], qts[g], preferred_element_type=F32)
             + jnp.dot(kml_ref[g], qts[g], preferred_element_type=F32) for g in range(heads)]
    for g in range(heads):
        gate = jnp.where(past, gates[g], -jnp.inf)
        rank = jnp.zeros((n_blocks, L), jnp.int32)
        for m in range(n_blocks):
            gm = gate[m:m + 1, :]
            ahead = jnp.logical_or(gm > gate, jnp.logical_and(gm == gate, blk_id > m))
            rank = rank + ahead.astype(jnp.int32)
        keep = jnp.logical_or(jnp.logical_and(rank < MOBA_TOPK, past), blk_id == i)
        mask_rows = jnp.where(keep, 0.0, NEG_INF)
        c2 = jnp.full((CONST_ROWS, L), slope_ref[hg * heads + g] * LOG2E, F32)
        c_hi = c2.astype(BF16).astype(F32)
        c_mid = (c2 - c_hi).astype(BF16).astype(F32)
        c_lo = c2 - c_hi - c_mid
        sel3 = part_row % 3
        const_rows = jnp.where(part_row < 6, jnp.where(sel3 == 0, c_hi, jnp.where(sel3 == 1, c_mid, c_lo)), 0.0)
        extra = jnp.concatenate(
            [const_rows, mask_rows, jnp.zeros((dh - CONST_ROWS - n_blocks, L), F32)], axis=0).astype(BF16)
        qa_ref[g, :dh, :] = qts[g]
        qa_ref[g, dh:, :] = extra

    def scores(g, off, blk):
        k_aug = jnp.concatenate([k_ref[pl.ds(off, L), lanes(g)], kx_ref[blk]], axis=1)
        return jnp.dot(k_aug, qa_ref[g], preferred_element_type=F32)

    own = [scores(g, base, i) for g in range(heads)]
    init = []
    for g in range(heads):
        s = jnp.where(causal, own[g], NEG_INF)
        m0 = jnp.max(s, axis=0, keepdims=True)
        p = jnp.exp2(s - m0)
        l0 = jnp.sum(p, axis=0, keepdims=True)
        acc0 = jnp.dot(vt_ref[g, :, pl.ds(base, L)], p.astype(BF16), preferred_element_type=F32)
        init += [m0, l0, acc0]

    def past_block(j, carry):
        off = pl.multiple_of(j * L, L)
        out = []
        raw = [scores(g, off, j) for g in range(heads)]
        for g in range(heads):
            m_prev, l_prev, acc = carry[3 * g:3 * g + 3]
            s = raw[g]
            m_new = jnp.maximum(m_prev, jnp.max(s, axis=0, keepdims=True))
            alpha = jnp.exp2(m_prev - m_new)
            p = jnp.exp2(s - m_new)
            l_new = alpha * l_prev + jnp.sum(p, axis=0, keepdims=True)
            pv = jnp.dot(vt_ref[g, :, pl.ds(off, L)], p.astype(BF16), preferred_element_type=F32)
            out += [m_new, l_new, alpha * acc + pv]
        return tuple(out)

    fin = lax.fori_loop(0, i, past_block, tuple(init))
    for g in range(heads):
        o_ref[:, lanes(g)] = (fin[3 * g + 2] / fin[3 * g + 1]).T.astype(BF16)


def _moba(proj, slopes, B, S, t):
    T = B * S
    nb = S // MOBA_BLOCK
    G = t["moba_heads_per_step"]
    per = GROUP // (G * MOBA_DIM)
    q_off, k_off, v_off = 3 * per, 4 * per, 5 * per
    return pl.pallas_call(
        functools.partial(_moba_kernel, n_blocks=nb, heads=G),
        grid=(B, MOBA_HEADS // G, nb),
        in_specs=[
            pl.BlockSpec(memory_space=pltpu.SMEM),
            pl.BlockSpec((MOBA_BLOCK, G * MOBA_DIM), lambda b, h, i: (b * nb + i, q_off + h)),
            pl.BlockSpec((S, G * MOBA_DIM), lambda b, h, i: (b, k_off + h)),
            pl.BlockSpec((S, G * MOBA_DIM), lambda b, h, i: (b, v_off + h)),
        ],
        out_specs=pl.BlockSpec((MOBA_BLOCK, G * MOBA_DIM), lambda b, h, i: (b * nb + i, h)),
        out_shape=jax.ShapeDtypeStruct((T, MOBA_HEADS * MOBA_DIM), BF16),
        scratch_shapes=[
            pltpu.VMEM((G, nb, MOBA_DIM), BF16),
            pltpu.VMEM((G, nb, MOBA_DIM), BF16),
            pltpu.VMEM((G, MOBA_DIM, S), BF16),
            pltpu.VMEM((nb, MOBA_BLOCK, MOBA_DIM), BF16),
            pltpu.VMEM((G, 2 * MOBA_DIM, MOBA_BLOCK), BF16),
        ],
        compiler_params=_params(("arbitrary", "arbitrary", "arbitrary"), t["vmem_limit"]),
    )(slopes, proj, proj, proj)


def _out_proj_kernel(yr_ref, ym_ref, wr_ref, wm_ref, x_ref, o_ref):
    o_ref[...] = (x_ref[...]
                  + jnp.dot(yr_ref[...], wr_ref[...], preferred_element_type=F32)
                  + jnp.dot(ym_ref[...], wm_ref[...], preferred_element_type=F32))


def _out_proj(y_r, y_m, w_out, layer, x2, t):
    T, D = x2.shape
    tm, tn = t["out_tm"], t["out_tn"]
    kr, km = y_r.shape[1], y_m.shape[1]
    return pl.pallas_call(
        _out_proj_kernel,
        grid=(T // tm, D // tn),
        in_specs=[
            pl.BlockSpec((tm, kr), lambda i, j: (i, 0)),
            pl.BlockSpec((tm, km), lambda i, j: (i, 0)),
            pl.BlockSpec((None, kr, tn), lambda i, j: (layer, 0, j)),
            pl.BlockSpec((None, km, tn), lambda i, j: (layer, kr // km, j)),
            pl.BlockSpec((tm, tn), lambda i, j: (i, j)),
        ],
        out_specs=pl.BlockSpec((tm, tn), lambda i, j: (i, j)),
        out_shape=jax.ShapeDtypeStruct((T, D), F32),
        compiler_params=_params(("arbitrary", "arbitrary"), t["vmem_limit"]),
    )(y_r, y_m, w_out, w_out, x2)


HALO = V7X_BF16_SUBLANES


def _ffn_kernel(x_ref, xh_ref, g_ref, wg_ref, wu_ref, cw_ref, cb_ref, wd_ref, o_ref, h_ref, *, tm, tiles_per_seq):
    i = pl.program_id(0)
    f = pl.program_id(1)

    def norm(x):
        ms = jnp.mean(x * x, axis=-1, keepdims=True)
        return (x * lax.rsqrt(ms + NORM_EPS) * g_ref[...]).astype(BF16)

    @pl.when(f == 0)
    def _():
        x = x_ref[...]
        h_ref[HALO:, :] = norm(x)
        seq_start = (i % tiles_per_seq) == 0
        h_ref[:HALO, :] = jnp.where(seq_start, jnp.zeros((HALO, x.shape[1]), BF16), norm(xh_ref[...]))
        o_ref[...] = x

    a = jnp.dot(h_ref[...], wg_ref[...], preferred_element_type=F32)
    u = jnp.dot(h_ref[HALO:, :], wu_ref[...], preferred_element_type=F32)
    conv = cb_ref[...]
    for tap in range(CONV_WIDTH):
        lo = HALO - (CONV_WIDTH - 1) + tap
        conv = conv + cw_ref[tap:tap + 1, :] * a[lo:lo + tm, :]
    act = (conv * jax.nn.sigmoid(conv) * u).astype(BF16)
    o_ref[...] += jnp.dot(act, wd_ref[...], preferred_element_type=F32)


def _conv_ffn(x2, ln_g, w_gate, w_up, conv_w, conv_b, w_down, layer, S, t):
    T, D = x2.shape
    F = w_gate.shape[2]
    tm, tf = t["ffn_tm"], t["ffn_tf"]
    halo_blocks = tm // HALO
    return pl.pallas_call(
        functools.partial(_ffn_kernel, tm=tm, tiles_per_seq=S // tm),
        grid=(T // tm, F // tf),
        in_specs=[
            pl.BlockSpec((tm, D), lambda i, f: (i, 0), pipeline_mode=pl.Buffered(1)),
            pl.BlockSpec((HALO, D), lambda i, f: (jnp.maximum(i * halo_blocks - 1, 0), 0)),
            pl.BlockSpec((1, D), lambda i, f: (0, 0)),
            pl.BlockSpec((None, D, tf), lambda i, f: (layer, 0, f)),
            pl.BlockSpec((None, D, tf), lambda i, f: (layer, 0, f)),
            pl.BlockSpec((CONV_WIDTH, tf), lambda i, f: (0, f)),
            pl.BlockSpec((1, tf), lambda i, f: (0, f)),
            pl.BlockSpec((None, tf, D), lambda i, f: (layer, f, 0)),
        ],
        out_specs=pl.BlockSpec((tm, D), lambda i, f: (i, 0)),
        out_shape=jax.ShapeDtypeStruct((T, D), F32),
        scratch_shapes=[pltpu.VMEM((HALO + tm, D), BF16)],
        compiler_params=_params(("arbitrary", "arbitrary"), t["vmem_limit"]),
    )(x2, x2, ln_g.reshape(1, D), w_gate, w_up, conv_w, conv_b.reshape(1, F), w_down)


def kernel(x, ln1_g, w_in, ret_norm_g, q_norm_g, k_norm_g, w_out, ln2_g, w_gate, w_up, conv_w, conv_b, w_down):
    B, S, D = x.shape
    depth = w_in.shape[0]
    t = _tiles()
    assert S % RET_CHUNK == 0 and S % MOBA_BLOCK == 0 and S % t["ffn_tm"] == 0
    assert w_in.shape[2] == 7 * GROUP

    log_decay = jnp.log(1.0 - 2.0 ** (-5.0 - jnp.arange(RET_HEADS, dtype=F32)))
    slopes = 2.0 ** (-8.0 * (jnp.arange(MOBA_HEADS, dtype=F32) + 1.0) / MOBA_HEADS)

    w_in, w_out, w_gate, w_up, w_down = (w.astype(BF16) for w in (w_in, w_out, w_gate, w_up, w_down))
    x2 = x.reshape(B * S, D)
    for l in range(depth):
        proj, gate = _in_proj(x2, ln1_g[l], w_in, l, q_norm_g[l], k_norm_g[l], t)
        y_r = _retention(proj, gate, ret_norm_g[l], log_decay, B, S, t)
        y_m = _moba(proj, slopes, B, S, t)
        x2 = _out_proj(y_r, y_m, w_out, l, x2, t)
        x2 = _conv_ffn(x2, ln2_g[l], w_gate, w_up, conv_w[l], conv_b[l], w_down, l, S, t)
    return x2.reshape(B, S, D)
```

```python
import functools

import jax
import jax.numpy as jnp
from jax import lax
from jax.experimental import pallas as pl
from jax.experimental.pallas import tpu as pltpu

RET_HEADS = 4
RET_DIM = 256
RET_CHUNK = 256
MOBA_HEADS = 8
MOBA_DIM = 128
MOBA_BLOCK = 256
MOBA_TOPK = 3
CONV_WIDTH = 3
NORM_EPS = 1e-6
NEG_INF = -1e30
LOG2E = 1.4426950408889634
GROUP = 1024

V7X_VMEM_BYTES = 64 * 1024 * 1024
V7X_BF16_SUBLANES = 16
V7X_MXU_COLS = 256

F32 = jnp.float32
BF16 = jnp.bfloat16


def _tiles():
    return dict(
        in_tm=1024, in_tn=1024,
        out_tm=1024, out_tn=1024,
        ffn_tm=1024, ffn_tf=512,
        moba_heads_per_step=8,
        ret_ts=1024,
        vmem_limit=int(V7X_VMEM_BYTES * 0.92),
    )


def _params(semantics, vmem_limit):
    return pltpu.CompilerParams(dimension_semantics=semantics, vmem_limit_bytes=vmem_limit)


_GATE_GROUP = 3
_MQ_GROUP = 4
_MK_GROUP = 5


def _in_proj_kernel(x_ref, g_ref, w_ref, qg_ref, kg_ref, o_ref, gate_ref, h_ref, *, tn):
    j = pl.program_id(1)
    per_group = GROUP // tn
    group = j // per_group
    sub = j % per_group
    cw = V7X_MXU_COLS
    cols = lambda c: slice(c * cw, (c + 1) * cw)

    @pl.when(j == 0)
    def _():
        x = x_ref[...]
        ms = jnp.mean(x * x, axis=-1, keepdims=True)
        h_ref[...] = (x * lax.rsqrt(ms + NORM_EPS) * g_ref[...]).astype(BF16)

    is_gate = group == _GATE_GROUP
    is_normed = jnp.logical_or(group == _MQ_GROUP, group == _MK_GROUP)

    @pl.when(is_gate)
    def _():
        for c in range(tn // cw):
            acc = jnp.dot(h_ref[...], w_ref[:, cols(c)], preferred_element_type=F32)
            gate_ref[:, pl.ds(pl.multiple_of(sub * tn + c * cw, cw), cw)] = acc

    @pl.when(jnp.logical_and(jnp.logical_not(is_gate), jnp.logical_not(is_normed)))
    def _():
        for c in range(tn // cw):
            acc = jnp.dot(h_ref[...], w_ref[:, cols(c)], preferred_element_type=F32)
            o_ref[:, cols(c)] = acc.astype(BF16)

    @pl.when(is_normed)
    def _():
        gain = jnp.where(group == _MQ_GROUP, qg_ref[...] * (MOBA_DIM ** -0.5 * LOG2E), kg_ref[...])
        for c in range(tn // cw):
            acc = jnp.dot(h_ref[...], w_ref[:, cols(c)], preferred_element_type=F32)
            for hh in range(cw // MOBA_DIM):
                seg = acc[:, hh * MOBA_DIM:(hh + 1) * MOBA_DIM]
                ms = jnp.mean(seg * seg, axis=-1, keepdims=True)
                lo = c * cw + hh * MOBA_DIM
                o_ref[:, lo:lo + MOBA_DIM] = (seg * lax.rsqrt(ms + NORM_EPS) * gain).astype(BF16)


def _in_proj(x2, ln_g, w_in, layer, q_g, k_g, t):
    T, D = x2.shape
    n_cols = w_in.shape[2]
    tm, tn = t["in_tm"], t["in_tn"]
    per_group = GROUP // tn
    gate_lo = _GATE_GROUP * per_group

    def o_map(i, j):
        return (i, jnp.where(j >= gate_lo + per_group, j - per_group, jnp.minimum(j, gate_lo - 1)))

    return pl.pallas_call(
        functools.partial(_in_proj_kernel, tn=tn),
        grid=(T // tm, n_cols // tn),
        in_specs=[
            pl.BlockSpec((tm, D), lambda i, j: (i, 0), pipeline_mode=pl.Buffered(1)),
            pl.BlockSpec((1, D), lambda i, j: (0, 0)),
            pl.BlockSpec((None, D, tn), lambda i, j: (layer, 0, j)),
            pl.BlockSpec((1, MOBA_DIM), lambda i, j: (0, 0)),
            pl.BlockSpec((1, MOBA_DIM), lambda i, j: (0, 0)),
        ],
        out_specs=[
            pl.BlockSpec((tm, tn), o_map),
            pl.BlockSpec((tm, GROUP), lambda i, j: (i, 0)),
        ],
        out_shape=[
            jax.ShapeDtypeStruct((T, n_cols - GROUP), BF16),
            jax.ShapeDtypeStruct((T, GROUP), F32),
        ],
        scratch_shapes=[pltpu.VMEM((tm, D), BF16)],
        compiler_params=_params(("arbitrary", "arbitrary"), t["vmem_limit"]),
    )(x2, ln_g.reshape(1, D), w_in, q_g.reshape(1, MOBA_DIM), k_g.reshape(1, MOBA_DIM))


def _retention_kernel(lg_ref, q_ref, k_ref, v_ref, gate_ref, g_ref, o_ref, state_ref, *, n_chunks):
    C = RET_CHUNK
    d = RET_DIM
    k_scale = d ** -0.5
    heads = range(RET_HEADS)
    cols = lambda h: slice(h * d, (h + 1) * d)
    nt = (((1,), (1,)), ((), ()))
    tn = (((0,), (0,)), ((), ()))

    @pl.when(pl.program_id(1) == 0)
    def _():
        state_ref[...] = jnp.zeros_like(state_ref)

    row = lax.broadcasted_iota(jnp.int32, (C, C), 0)
    col = lax.broadcasted_iota(jnp.int32, (C, C), 1)
    diff = (row - col).astype(F32)
    pos = lax.broadcasted_iota(jnp.int32, (C, 1), 0).astype(F32)
    inner_decay, q_decay, k_decay, chunk_decay = [], [], [], []
    for h in heads:
        lg = lg_ref[h]
        inner_decay.append(jnp.where(diff >= 0, jnp.exp(lg * jnp.maximum(diff, 0.0)), 0.0) * k_scale)
        q_decay.append(jnp.exp(lg * (pos + 1.0)))
        k_decay.append(jnp.exp(lg * (C - 1.0 - pos)) * k_scale)
        chunk_decay.append(jnp.exp(jnp.full((1, 1), lg * C, F32)))

    def chunk(n, carry):
        sl = pl.ds(pl.multiple_of(n * C, C), C)
        q = [q_ref[sl, cols(h)] for h in heads]
        k = [k_ref[sl, cols(h)] for h in heads]
        v = [v_ref[sl, cols(h)] for h in heads]
        state = [state_ref[h] for h in heads]
        raw = [lax.dot_general(q[h], k[h], nt, preferred_element_type=F32) for h in heads]
        cross = [jnp.dot(q[h], state[h].astype(BF16), preferred_element_type=F32) for h in heads]
        inner = [jnp.dot((raw[h] * inner_decay[h]).astype(BF16), v[h], preferred_element_type=F32) for h in heads]
        for h in heads:
            kd = (k[h].astype(F32) * k_decay[h]).astype(BF16)
            kv = lax.dot_general(kd, v[h], tn, preferred_element_type=F32)
            state_ref[h] = state[h] * chunk_decay[h] + kv
        for h in heads:
            y = inner[h] + cross[h] * q_decay[h]
            ms = jnp.mean(y * y, axis=-1, keepdims=True)
            y = y * lax.rsqrt(ms + NORM_EPS) * g_ref[h:h + 1, :]
            gt = gate_ref[sl, cols(h)]
            o_ref[sl, cols(h)] = (y * (gt * jax.nn.sigmoid(gt))).astype(BF16)
        return carry

    lax.fori_loop(0, n_chunks, chunk, 0)


def _retention(proj, gate, ret_g, log_decay, B, S, t):
    T = B * S
    ts = t["ret_ts"]
    width = RET_HEADS * RET_DIM
    assert width == GROUP and S % ts == 0 and ts % RET_CHUNK == 0
    tiles = S // ts
    blk = lambda group: pl.BlockSpec((ts, width), lambda b, s: (b * tiles + s, group))
    return pl.pallas_call(
        functools.partial(_retention_kernel, n_chunks=ts // RET_CHUNK),
        grid=(B, tiles),
        in_specs=[
            pl.BlockSpec(memory_space=pltpu.SMEM),
            blk(0), blk(1), blk(2),
            pl.BlockSpec((ts, width), lambda b, s: (b * tiles + s, 0)),
            pl.BlockSpec((RET_HEADS, RET_DIM), lambda b, s: (0, 0)),
        ],
        out_specs=pl.BlockSpec((ts, width), lambda b, s: (b * tiles + s, 0)),
        out_shape=jax.ShapeDtypeStruct((T, width), BF16),
        scratch_shapes=[pltpu.VMEM((RET_HEADS, RET_DIM, RET_DIM), F32)],
        compiler_params=_params(("arbitrary", "arbitrary"), t["vmem_limit"]),
    )(log_decay, proj, proj, proj, gate, ret_g)


def _moba_kernel(slope_ref, q_ref, k_ref, v_ref, o_ref, kmh_ref, kml_ref, vt_ref, kx_ref, qa_ref,
                 *, n_blocks, heads):
    hg = pl.program_id(1)
    i = pl.program_id(2)
    L = MOBA_BLOCK
    dh = MOBA_DIM
    lanes = lambda g: slice(g * dh, (g + 1) * dh)
    CONST_ROWS = 8
    MASK_ROW0 = CONST_ROWS

    @pl.when(i == 0)
    def _():
        lane = lax.broadcasted_iota(jnp.int32, (L, dh), 1)
        key = lax.broadcasted_iota(jnp.int32, (L, dh), 0)
        for n in range(n_blocks):
            kx = jnp.where(lane < 3, key, jnp.where(lane < 6, n * L, jnp.where(lane == MASK_ROW0 + n, 1, 0)))
            kx_ref[n * L:(n + 1) * L, :] = kx.astype(F32).astype(BF16)
        for g in range(heads):
            for n in range(n_blocks):
                rows = slice(n * L, (n + 1) * L)
                kb = k_ref[rows, lanes(g)].astype(F32)
                mean = jnp.sum(kb, axis=0, keepdims=True) * (1.0 / L)
                hi = mean.astype(BF16)
                kmh_ref[g, n:n + 1, :] = hi
                kml_ref[g, n:n + 1, :] = (mean - hi.astype(F32)).astype(BF16)
                vt_ref[g, :, rows] = v_ref[rows, lanes(g)].astype(F32).T.astype(BF16)

    blk_id = lax.broadcasted_iota(jnp.int32, (n_blocks, L), 0)
    past = blk_id < i
    base = pl.multiple_of(i * L, L)
    part_row = lax.broadcasted_iota(jnp.int32, (CONST_ROWS, L), 0)
    causal = (lax.broadcasted_iota(jnp.int32, (L, L), 1) >= lax.broadcasted_iota(jnp.int32, (L, L), 0))

    qts = [q_ref[:, lanes(g)].astype(F32).T.astype(BF16) for g in range(heads)]
    gates = [jnp.dot(kmh_ref[g], qts[g], preferred_element_type=F32)
             + jnp.dot(kml_ref[g], qts[g], preferred_element_type=F32) for g in range(heads)]
    for g in range(heads):
        gate = jnp.where(past, gates[g], -jnp.inf)
        rank = jnp.zeros((n_blocks, L), jnp.int32)
        for m in range(n_blocks):
            gm = gate[m:m + 1, :]
            ahead = jnp.logical_or(gm > gate, jnp.logical_and(gm == gate, blk_id > m))
            rank = rank + ahead.astype(jnp.int32)
        mask_rows = jnp.where(jnp.logical_and(rank < MOBA_TOPK, past), 0.0, NEG_INF)
        c2 = jnp.full((CONST_ROWS, L), slope_ref[hg * heads + g] * LOG2E, F32)
        c_hi = c2.astype(BF16).astype(F32)
        c_mid = (c2 - c_hi).astype(BF16).astype(F32)
        c_lo = c2 - c_hi - c_mid
        sel3 = part_row % 3
        const_rows = jnp.where(part_row < 6, jnp.where(sel3 == 0, c_hi, jnp.where(sel3 == 1, c_mid, c_lo)), 0.0)
        extra = jnp.concatenate(
            [const_rows, mask_rows, jnp.zeros((dh - CONST_ROWS - n_blocks, L), F32)], axis=0).astype(BF16)
        qa_ref[g, :dh, :] = qts[g]
        qa_ref[g, dh:, :] = extra

    def scores(g, off, n_keys, kx):
        k_aug = jnp.concatenate([k_ref[pl.ds(off, n_keys), lanes(g)], kx], axis=1)
        return jnp.dot(k_aug, qa_ref[g], preferred_element_type=F32)

    def pair_scores(jj):
        off = pl.multiple_of(jj * (2 * L), 2 * L)
        kx = kx_ref[pl.ds(off, 2 * L), :]
        return [scores(g, off, 2 * L, kx) for g in range(heads)]

    kx_lane = lax.broadcasted_iota(jnp.int32, (L, dh), 1)
    kx_own = jnp.where(kx_lane < MASK_ROW0, kx_ref[pl.ds(base, L), :].astype(F32), 0.0).astype(BF16)
    own = [scores(g, base, L, kx_own) for g in range(heads)]
    init = []
    for g in range(heads):
        s = jnp.where(causal, own[g], NEG_INF)
        m0 = jnp.max(s, axis=0, keepdims=True)
        p = jnp.exp2(s - m0)
        l0 = jnp.sum(p, axis=0, keepdims=True)
        acc0 = jnp.dot(vt_ref[g, :, pl.ds(base, L)], p.astype(BF16), preferred_element_type=F32)
        init += [m0, l0, acc0]

    def past_pair(jj, carry):
        off = pl.multiple_of(jj * (2 * L), 2 * L)
        raw = pair_scores(jj)
        out = []
        for g in range(heads):
            m_prev, l_prev, acc = carry[3 * g:3 * g + 3]
            s = raw[g]
            m_new = jnp.maximum(m_prev, jnp.max(s, axis=0, keepdims=True))
            alpha = jnp.exp2(m_prev - m_new)
            p = jnp.exp2(s - m_new)
            l_new = alpha * l_prev + jnp.sum(p, axis=0, keepdims=True)
            pv = jnp.dot(vt_ref[g, :, pl.ds(off, 2 * L)], p.astype(BF16), preferred_element_type=F32)
            out += [m_new, l_new, alpha * acc + pv]
        return tuple(out)

    fin = lax.fori_loop(0, (i + 1) // 2, past_pair, tuple(init))
    for g in range(heads):
        o_ref[:, lanes(g)] = (fin[3 * g + 2] / fin[3 * g + 1]).T.astype(BF16)


def _moba(proj, slopes, B, S, t):
    T = B * S
    nb = S // MOBA_BLOCK
    G = t["moba_heads_per_step"]
    per = GROUP // (G * MOBA_DIM)
    q_off, k_off, v_off = 3 * per, 4 * per, 5 * per
    return pl.pallas_call(
        functools.partial(_moba_kernel, n_blocks=nb, heads=G),
        grid=(B, MOBA_HEADS // G, nb),
        in_specs=[
            pl.BlockSpec(memory_space=pltpu.SMEM),
            pl.BlockSpec((MOBA_BLOCK, G * MOBA_DIM), lambda b, h, i: (b * nb + i, q_off + h)),
            pl.BlockSpec((S, G * MOBA_DIM), lambda b, h, i: (b, k_off + h)),
            pl.BlockSpec((S, G * MOBA_DIM), lambda b, h, i: (b, v_off + h)),
        ],
        out_specs=pl.BlockSpec((MOBA_BLOCK, G * MOBA_DIM), lambda b, h, i: (b * nb + i, h)),
        out_shape=jax.ShapeDtypeStruct((T, MOBA_HEADS * MOBA_DIM), BF16),
        scratch_shapes=[
            pltpu.VMEM((G, nb, MOBA_DIM), BF16),
            pltpu.VMEM((G, nb, MOBA_DIM), BF16),
            pltpu.VMEM((G, MOBA_DIM, S), BF16),
            pltpu.VMEM((S, MOBA_DIM), BF16),
            pltpu.VMEM((G, 2 * MOBA_DIM, MOBA_BLOCK), BF16),
        ],
        compiler_params=_params(("arbitrary", "arbitrary", "arbitrary"), t["vmem_limit"]),
    )(slopes, proj, proj, proj)


def _out_proj_kernel(yr_ref, ym_ref, wr_ref, wm_ref, x_ref, o_ref):
    o_ref[...] = (x_ref[...]
                  + jnp.dot(yr_ref[...], wr_ref[...], preferred_element_type=F32)
                  + jnp.dot(ym_ref[...], wm_ref[...], preferred_element_type=F32))


def _out_proj(y_r, y_m, w_out, layer, x2, t):
    T, D = x2.shape
    tm, tn = t["out_tm"], t["out_tn"]
    kr, km = y_r.shape[1], y_m.shape[1]
    return pl.pallas_call(
        _out_proj_kernel,
        grid=(T // tm, D // tn),
        in_specs=[
            pl.BlockSpec((tm, kr), lambda i, j: (i, 0)),
            pl.BlockSpec((tm, km), lambda i, j: (i, 0)),
            pl.BlockSpec((None, kr, tn), lambda i, j: (layer, 0, j)),
            pl.BlockSpec((None, km, tn), lambda i, j: (layer, kr // km, j)),
            pl.BlockSpec((tm, tn), lambda i, j: (i, j)),
        ],
        out_specs=pl.BlockSpec((tm, tn), lambda i, j: (i, j)),
        out_shape=jax.ShapeDtypeStruct((T, D), F32),
        compiler_params=_params(("arbitrary", "arbitrary"), t["vmem_limit"]),
    )(y_r, y_m, w_out, w_out, x2)


HALO = V7X_BF16_SUBLANES


def _ffn_kernel(x_ref, xh_ref, g_ref, wg_ref, wu_ref, cw_ref, cb_ref, wd_ref, o_ref, h_ref, *, tm, tiles_per_seq):
    i = pl.program_id(0)
    f = pl.program_id(1)

    def norm(x):
        ms = jnp.mean(x * x, axis=-1, keepdims=True)
        return (x * lax.rsqrt(ms + NORM_EPS) * g_ref[...]).astype(BF16)

    @pl.when(f == 0)
    def _():
        x = x_ref[...]
        h_ref[HALO:, :] = norm(x)
        seq_start = (i % tiles_per_seq) == 0
        h_ref[:HALO, :] = jnp.where(seq_start, jnp.zeros((HALO, x.shape[1]), BF16), norm(xh_ref[...]))
        o_ref[...] = x

    a = jnp.dot(h_ref[...], wg_ref[...], preferred_element_type=F32)
    u = jnp.dot(h_ref[HALO:, :], wu_ref[...], preferred_element_type=F32)
    conv = cb_ref[...]
    for tap in range(CONV_WIDTH):
        lo = HALO - (CONV_WIDTH - 1) + tap
        conv = conv + cw_ref[tap:tap + 1, :] * a[lo:lo + tm, :]
    act = (conv * jax.nn.sigmoid(conv) * u).astype(BF16)
    o_ref[...] += jnp.dot(act, wd_ref[...], preferred_element_type=F32)


def _conv_ffn(x2, ln_g, w_gate, w_up, conv_w, conv_b, w_down, layer, S, t):
    T, D = x2.shape
    F = w_gate.shape[2]
    tm, tf = t["ffn_tm"], t["ffn_tf"]
    halo_blocks = tm // HALO
    return pl.pallas_call(
        functools.partial(_ffn_kernel, tm=tm, tiles_per_seq=S // tm),
        grid=(T // tm, F // tf),
        in_specs=[
            pl.BlockSpec((tm, D), lambda i, f: (i, 0), pipeline_mode=pl.Buffered(1)),
            pl.BlockSpec((HALO, D), lambda i, f: (jnp.maximum(i * halo_blocks - 1, 0), 0)),
            pl.BlockSpec((1, D), lambda i, f: (0, 0)),
            pl.BlockSpec((None, D, tf), lambda i, f: (layer, 0, f)),
            pl.BlockSpec((None, D, tf), lambda i, f: (layer, 0, f)),
            pl.BlockSpec((CONV_WIDTH, tf), lambda i, f: (0, f)),
            pl.BlockSpec((1, tf), lambda i, f: (0, f)),
            pl.BlockSpec((None, tf, D), lambda i, f: (layer, f, 0)),
        ],
        out_specs=pl.BlockSpec((tm, D), lambda i, f: (i, 0)),
        out_shape=jax.ShapeDtypeStruct((T, D), F32),
        scratch_shapes=[pltpu.VMEM((HALO + tm, D), BF16)],
        compiler_params=_params(("arbitrary", "arbitrary"), t["vmem_limit"]),
    )(x2, x2, ln_g.reshape(1, D), w_gate, w_up, conv_w, conv_b.reshape(1, F), w_down)


def kernel(x, ln1_g, w_in, ret_norm_g, q_norm_g, k_norm_g, w_out, ln2_g, w_gate, w_up, conv_w, conv_b, w_down):
    B, S, D = x.shape
    depth = w_in.shape[0]
    t = _tiles()
    assert S % RET_CHUNK == 0 and S % MOBA_BLOCK == 0 and S % t["ffn_tm"] == 0
    assert w_in.shape[2] == 7 * GROUP

    log_decay = jnp.log(1.0 - 2.0 ** (-5.0 - jnp.arange(RET_HEADS, dtype=F32)))
    slopes = 2.0 ** (-8.0 * (jnp.arange(MOBA_HEADS, dtype=F32) + 1.0) / MOBA_HEADS)

    w_in, w_out, w_gate, w_up, w_down = (w.astype(BF16) for w in (w_in, w_out, w_gate, w_up, w_down))
    x2 = x.reshape(B * S, D)
    for l in range(depth):
        proj, gate = _in_proj(x2, ln1_g[l], w_in, l, q_norm_g[l], k_norm_g[l], t)
        y_r = _retention(proj, gate, ret_norm_g[l], log_decay, B, S, t)
        y_m = _moba(proj, slopes, B, S, t)
        x2 = _out_proj(y_r, y_m, w_out, l, x2, t)
        x2 = _conv_ffn(x2, ln2_g[l], w_gate, w_up, conv_w[l], conv_b[l], w_down, l, S, t)
    return x2.reshape(B, S, D)
```

```python
import functools

import jax
import jax.numpy as jnp
from jax import lax
from jax.experimental import pallas as pl
from jax.experimental.pallas import tpu as pltpu

RET_HEADS = 4
RET_DIM = 256
RET_CHUNK = 256
MOBA_HEADS = 8
MOBA_DIM = 128
MOBA_BLOCK = 256
MOBA_TOPK = 3
CONV_WIDTH = 3
NORM_EPS = 1e-6
NEG_INF = -1e30
LOG2E = 1.4426950408889634
GROUP = 1024

V7X_VMEM_BYTES = 64 * 1024 * 1024
V7X_BF16_SUBLANES = 16
V7X_MXU_COLS = 256

F32 = jnp.float32
BF16 = jnp.bfloat16


def _tiles():
    return dict(
        in_tm=512,
        out_tm=1024,
        ffn_tm=1024, ffn_tf=512,
        moba_heads_per_step=8,
        ret_ts=1024,
        vmem_limit=int(V7X_VMEM_BYTES * 0.92),
    )


def _params(semantics, vmem_limit):
    return pltpu.CompilerParams(dimension_semantics=semantics, vmem_limit_bytes=vmem_limit)


def _vmem_request(block_bytes):
    return min(block_bytes + V7X_VMEM_BYTES // 16, V7X_VMEM_BYTES - V7X_VMEM_BYTES // 32)


_GATE_GROUP = 3
_MQ_GROUP = 4
_MK_GROUP = 5


def _in_proj_kernel(x_ref, g_ref, w_ref, qg_ref, kg_ref, o_ref, gate_ref, h_ref):
    x = x_ref[...]
    ms = jnp.mean(x * x, axis=-1, keepdims=True)
    h_ref[...] = (x * lax.rsqrt(ms + NORM_EPS) * g_ref[...]).astype(BF16)

    cw = V7X_MXU_COLS
    gains = {_MQ_GROUP: qg_ref[...] * (MOBA_DIM ** -0.5 * LOG2E), _MK_GROUP: kg_ref[...]}
    for group in range(w_ref.shape[1] // GROUP):
        out_group = group - (group > _GATE_GROUP)
        for c in range(GROUP // cw):
            lo = c * cw
            acc = jnp.dot(h_ref[...], w_ref[:, group * GROUP + lo:group * GROUP + lo + cw],
                          preferred_element_type=F32)
            if group == _GATE_GROUP:
                gate_ref[:, lo:lo + cw] = acc
            elif group in gains:
                for hh in range(cw // MOBA_DIM):
                    seg = acc[:, hh * MOBA_DIM:(hh + 1) * MOBA_DIM]
                    ms = jnp.mean(seg * seg, axis=-1, keepdims=True)
                    at = out_group * GROUP + lo + hh * MOBA_DIM
                    o_ref[:, at:at + MOBA_DIM] = (seg * lax.rsqrt(ms + NORM_EPS) * gains[group]).astype(BF16)
            else:
                o_ref[:, out_group * GROUP + lo:out_group * GROUP + lo + cw] = acc.astype(BF16)


def _in_proj(x2, ln_g, w_in, layer, q_g, k_g, t):
    T, D = x2.shape
    n_cols = w_in.shape[2]
    tm = t["in_tm"]
    n_out = n_cols - GROUP
    vmem = D * n_cols * 2 + 2 * tm * D * 4 + 2 * tm * n_out * 2 + 2 * tm * GROUP * 4 + tm * D * 2
    return pl.pallas_call(
        _in_proj_kernel,
        grid=(T // tm,),
        in_specs=[
            pl.BlockSpec((tm, D), lambda i: (i, 0)),
            pl.BlockSpec((1, D), lambda i: (0, 0)),
            pl.BlockSpec((None, D, n_cols), lambda i: (layer, 0, 0), pipeline_mode=pl.Buffered(1)),
            pl.BlockSpec((1, MOBA_DIM), lambda i: (0, 0)),
            pl.BlockSpec((1, MOBA_DIM), lambda i: (0, 0)),
        ],
        out_specs=[
            pl.BlockSpec((tm, n_out), lambda i: (i, 0)),
            pl.BlockSpec((tm, GROUP), lambda i: (i, 0)),
        ],
        out_shape=[
            jax.ShapeDtypeStruct((T, n_out), BF16),
            jax.ShapeDtypeStruct((T, GROUP), F32),
        ],
        scratch_shapes=[pltpu.VMEM((tm, D), BF16)],
        compiler_params=_params(("arbitrary",), _vmem_request(vmem)),
    )(x2, ln_g.reshape(1, D), w_in, q_g.reshape(1, MOBA_DIM), k_g.reshape(1, MOBA_DIM))


def _retention_kernel(lg_ref, q_ref, k_ref, v_ref, gate_ref, g_ref, o_ref, state_ref, *, n_chunks):
    C = RET_CHUNK
    d = RET_DIM
    k_scale = d ** -0.5
    heads = range(RET_HEADS)
    cols = lambda h: slice(h * d, (h + 1) * d)
    nt = (((1,), (1,)), ((), ()))
    tn = (((0,), (0,)), ((), ()))

    @pl.when(pl.program_id(1) == 0)
    def _():
        state_ref[...] = jnp.zeros_like(state_ref)

    row = lax.broadcasted_iota(jnp.int32, (C, C), 0)
    col = lax.broadcasted_iota(jnp.int32, (C, C), 1)
    diff = (row - col).astype(F32)
    pos = lax.broadcasted_iota(jnp.int32, (C, 1), 0).astype(F32)
    inner_decay, q_decay, k_decay, chunk_decay = [], [], [], []
    for h in heads:
        lg = lg_ref[h]
        inner_decay.append(jnp.where(diff >= 0, jnp.exp(lg * jnp.maximum(diff, 0.0)), 0.0) * k_scale)
        q_decay.append(jnp.exp(lg * (pos + 1.0)))
        k_decay.append(jnp.exp(lg * (C - 1.0 - pos)) * k_scale)
        chunk_decay.append(jnp.exp(jnp.full((1, 1), lg * C, F32)))

    def chunk(n, carry):
        sl = pl.ds(pl.multiple_of(n * C, C), C)
        q = [q_ref[sl, cols(h)] for h in heads]
        k = [k_ref[sl, cols(h)] for h in heads]
        v = [v_ref[sl, cols(h)] for h in heads]
        state = [state_ref[h] for h in heads]
        raw = [lax.dot_general(q[h], k[h], nt, preferred_element_type=F32) for h in heads]
        cross = [jnp.dot(q[h], state[h].astype(BF16), preferred_element_type=F32) for h in heads]
        inner = [jnp.dot((raw[h] * inner_decay[h]).astype(BF16), v[h], preferred_element_type=F32) for h in heads]
        for h in heads:
            kd = (k[h].astype(F32) * k_decay[h]).astype(BF16)
            kv = lax.dot_general(kd, v[h], tn, preferred_element_type=F32)
            state_ref[h] = state[h] * chunk_decay[h] + kv
        for h in heads:
            y = inner[h] + cross[h] * q_decay[h]
            ms = jnp.mean(y * y, axis=-1, keepdims=True)
            y = y * lax.rsqrt(ms + NORM_EPS) * g_ref[h:h + 1, :]
            gt = gate_ref[sl, cols(h)]
            o_ref[sl, cols(h)] = (y * (gt * jax.nn.sigmoid(gt))).astype(BF16)
        return carry

    lax.fori_loop(0, n_chunks, chunk, 0)


def _retention(proj, gate, ret_g, log_decay, B, S, t):
    T = B * S
    ts = t["ret_ts"]
    width = RET_HEADS * RET_DIM
    assert width == GROUP and S % ts == 0 and ts % RET_CHUNK == 0
    tiles = S // ts
    blk = lambda group: pl.BlockSpec((ts, width), lambda b, s: (b * tiles + s, group))
    return pl.pallas_call(
        functools.partial(_retention_kernel, n_chunks=ts // RET_CHUNK),
        grid=(B, tiles),
        in_specs=[
            pl.BlockSpec(memory_space=pltpu.SMEM),
            blk(0), blk(1), blk(2),
            pl.BlockSpec((ts, width), lambda b, s: (b * tiles + s, 0)),
            pl.BlockSpec((RET_HEADS, RET_DIM), lambda b, s: (0, 0)),
        ],
        out_specs=pl.BlockSpec((ts, width), lambda b, s: (b * tiles + s, 0)),
        out_shape=jax.ShapeDtypeStruct((T, width), BF16),
        scratch_shapes=[pltpu.VMEM((RET_HEADS, RET_DIM, RET_DIM), F32)],
        compiler_params=_params(("arbitrary", "arbitrary"), t["vmem_limit"]),
    )(log_decay, proj, proj, proj, gate, ret_g)


def _moba_kernel(slope_ref, q_ref, k_ref, v_ref, o_ref, kmh_ref, kml_ref, vt_ref, kx_ref, qa_ref,
                 *, n_blocks, heads):
    hg = pl.program_id(1)
    i = pl.program_id(2)
    L = MOBA_BLOCK
    dh = MOBA_DIM
    lanes = lambda g: slice(g * dh, (g + 1) * dh)
    CONST_ROWS = 8
    MASK_ROW0 = CONST_ROWS

    @pl.when(i == 0)
    def _():
        lane = lax.broadcasted_iota(jnp.int32, (L, dh), 1)
        key = lax.broadcasted_iota(jnp.int32, (L, dh), 0)
        for n in range(n_blocks):
            kx = jnp.where(lane < 3, key, jnp.where(lane < 6, n * L, jnp.where(lane == MASK_ROW0 + n, 1, 0)))
            kx_ref[n * L:(n + 1) * L, :] = kx.astype(F32).astype(BF16)
        for g in range(heads):
            for n in range(n_blocks):
                rows = slice(n * L, (n + 1) * L)
                kb = k_ref[rows, lanes(g)].astype(F32)
                mean = jnp.sum(kb, axis=0, keepdims=True) * (1.0 / L)
                hi = mean.astype(BF16)
                kmh_ref[g, n:n + 1, :] = hi
                kml_ref[g, n:n + 1, :] = (mean - hi.astype(F32)).astype(BF16)
                vt_ref[g, :, rows] = v_ref[rows, lanes(g)].astype(F32).T.astype(BF16)

    blk_id = lax.broadcasted_iota(jnp.int32, (n_blocks, L), 0)
    past = blk_id < i
    base = pl.multiple_of(i * L, L)
    part_row = lax.broadcasted_iota(jnp.int32, (CONST_ROWS, L), 0)
    causal = (lax.broadcasted_iota(jnp.int32, (L, L), 1) >= lax.broadcasted_iota(jnp.int32, (L, L), 0))

    qts = [q_ref[:, lanes(g)].astype(F32).T.astype(BF16) for g in range(heads)]
    gates = [jnp.dot(kmh_ref[g], qts[g], preferred_element_type=F32)
             + jnp.dot(kml_ref[g], qts[g], preferred_element_type=F32) for g in range(heads)]
    for g in range(heads):
        gate = jnp.where(past, gates[g], -jnp.inf)
        rank = jnp.zeros((n_blocks, L), jnp.int32)
        for m in range(n_blocks):
            gm = gate[m:m + 1, :]
            ahead = jnp.logical_or(gm > gate, jnp.logical_and(gm == gate, blk_id > m))
            rank = rank + ahead.astype(jnp.int32)
        mask_rows = jnp.where(jnp.logical_and(rank < MOBA_TOPK, past), 0.0, NEG_INF)
        c2 = jnp.full((CONST_ROWS, L), slope_ref[hg * heads + g] * LOG2E, F32)
        c_hi = c2.astype(BF16).astype(F32)
        c_mid = (c2 - c_hi).astype(BF16).astype(F32)
        c_lo = c2 - c_hi - c_mid
        sel3 = part_row % 3
        const_rows = jnp.where(part_row < 6, jnp.where(sel3 == 0, c_hi, jnp.where(sel3 == 1, c_mid, c_lo)), 0.0)
        extra = jnp.concatenate(
            [const_rows, mask_rows, jnp.zeros((dh - CONST_ROWS - n_blocks, L), F32)], axis=0).astype(BF16)
        qa_ref[g, :dh, :] = qts[g]
        qa_ref[g, dh:, :] = extra

    def scores(g, off, n_keys, kx):
        k_aug = jnp.concatenate([k_ref[pl.ds(off, n_keys), lanes(g)], kx], axis=1)
        return jnp.dot(k_aug, qa_ref[g], preferred_element_type=F32)

    def pair_scores(jj):
        off = pl.multiple_of(jj * (2 * L), 2 * L)
        kx = kx_ref[pl.ds(off, 2 * L), :]
        return [scores(g, off, 2 * L, kx) for g in range(heads)]

    kx_lane = lax.broadcasted_iota(jnp.int32, (L, dh), 1)
    kx_own = jnp.where(kx_lane < MASK_ROW0, kx_ref[pl.ds(base, L), :].astype(F32), 0.0).astype(BF16)
    own = [scores(g, base, L, kx_own) for g in range(heads)]
    init = []
    for g in range(heads):
        s = jnp.where(causal, own[g], NEG_INF)
        m0 = jnp.max(s, axis=0, keepdims=True)
        p = jnp.exp2(s - m0)
        l0 = jnp.sum(p, axis=0, keepdims=True)
        acc0 = jnp.dot(vt_ref[g, :, pl.ds(base, L)], p.astype(BF16), preferred_element_type=F32)
        init += [m0, l0, acc0]

    def past_pair(jj, carry):
        off = pl.multiple_of(jj * (2 * L), 2 * L)
        raw = pair_scores(jj)
        out = []
        for g in range(heads):
            m_prev, l_prev, acc = carry[3 * g:3 * g + 3]
            s = raw[g]
            m_new = jnp.maximum(m_prev, jnp.max(s, axis=0, keepdims=True))
            alpha = jnp.exp2(m_prev - m_new)
            p = jnp.exp2(s - m_new)
            l_new = alpha * l_prev + jnp.sum(p, axis=0, keepdims=True)
            pv = jnp.dot(vt_ref[g, :, pl.ds(off, 2 * L)], p.astype(BF16), preferred_element_type=F32)
            out += [m_new, l_new, alpha * acc + pv]
        return tuple(out)

    fin = lax.fori_loop(0, (i + 1) // 2, past_pair, tuple(init))
    for g in range(heads):
        o_ref[:, lanes(g)] = (fin[3 * g + 2] / fin[3 * g + 1]).T.astype(BF16)


def _moba(proj, slopes, B, S, t):
    T = B * S
    nb = S // MOBA_BLOCK
    G = t["moba_heads_per_step"]
    per = GROUP // (G * MOBA_DIM)
    q_off, k_off, v_off = 3 * per, 4 * per, 5 * per
    return pl.pallas_call(
        functools.partial(_moba_kernel, n_blocks=nb, heads=G),
        grid=(B, MOBA_HEADS // G, nb),
        in_specs=[
            pl.BlockSpec(memory_space=pltpu.SMEM),
            pl.BlockSpec((MOBA_BLOCK, G * MOBA_DIM), lambda b, h, i: (b * nb + i, q_off + h)),
            pl.BlockSpec((S, G * MOBA_DIM), lambda b, h, i: (b, k_off + h)),
            pl.BlockSpec((S, G * MOBA_DIM), lambda b, h, i: (b, v_off + h)),
        ],
        out_specs=pl.BlockSpec((MOBA_BLOCK, G * MOBA_DIM), lambda b, h, i: (b * nb + i, h)),
        out_shape=jax.ShapeDtypeStruct((T, MOBA_HEADS * MOBA_DIM), BF16),
        scratch_shapes=[
            pltpu.VMEM((G, nb, MOBA_DIM), BF16),
            pltpu.VMEM((G, nb, MOBA_DIM), BF16),
            pltpu.VMEM((G, MOBA_DIM, S), BF16),
            pltpu.VMEM((S, MOBA_DIM), BF16),
            pltpu.VMEM((G, 2 * MOBA_DIM, MOBA_BLOCK), BF16),
        ],
        compiler_params=_params(("arbitrary", "arbitrary", "arbitrary"), t["vmem_limit"]),
    )(slopes, proj, proj, proj)


def _out_proj_kernel(yr_ref, ym_ref, w_ref, x_ref, o_ref):
    kr = yr_ref.shape[1]
    cw = V7X_MXU_COLS
    for c in range(o_ref.shape[1] // cw):
        cols = slice(c * cw, (c + 1) * cw)
        o_ref[:, cols] = (x_ref[:, cols]
                          + jnp.dot(yr_ref[...], w_ref[:kr, cols], preferred_element_type=F32)
                          + jnp.dot(ym_ref[...], w_ref[kr:, cols], preferred_element_type=F32))


def _out_proj(y_r, y_m, w_out, layer, x2, t):
    T, D = x2.shape
    tm = t["out_tm"]
    kr, km = y_r.shape[1], y_m.shape[1]
    vmem = (kr + km) * D * 2 + 2 * tm * (kr + km) * 2 + 4 * tm * D * 4
    return pl.pallas_call(
        _out_proj_kernel,
        grid=(T // tm,),
        in_specs=[
            pl.BlockSpec((tm, kr), lambda i: (i, 0)),
            pl.BlockSpec((tm, km), lambda i: (i, 0)),
            pl.BlockSpec((None, kr + km, D), lambda i: (layer, 0, 0), pipeline_mode=pl.Buffered(1)),
            pl.BlockSpec((tm, D), lambda i: (i, 0)),
        ],
        out_specs=pl.BlockSpec((tm, D), lambda i: (i, 0)),
        out_shape=jax.ShapeDtypeStruct((T, D), F32),
        compiler_params=_params(("arbitrary",), _vmem_request(vmem)),
    )(y_r, y_m, w_out, x2)


HALO = V7X_BF16_SUBLANES


def _ffn_kernel(x_ref, xh_ref, g_ref, wg_ref, wu_ref, cw_ref, cb_ref, wd_ref, o_ref, h_ref, *, tm, tiles_per_seq):
    i = pl.program_id(0)
    f = pl.program_id(1)

    def norm(x):
        ms = jnp.mean(x * x, axis=-1, keepdims=True)
        return (x * lax.rsqrt(ms + NORM_EPS) * g_ref[...]).astype(BF16)

    @pl.when(f == 0)
    def _():
        x = x_ref[...]
        h_ref[HALO:, :] = norm(x)
        seq_start = (i % tiles_per_seq) == 0
        h_ref[:HALO, :] = jnp.where(seq_start, jnp.zeros((HALO, x.shape[1]), BF16), norm(xh_ref[...]))
        o_ref[...] = x

    a = jnp.dot(h_ref[...], wg_ref[...], preferred_element_type=F32)
    u = jnp.dot(h_ref[HALO:, :], wu_ref[...], preferred_element_type=F32)
    conv = cb_ref[...]
    for tap in range(CONV_WIDTH):
        lo = HALO - (CONV_WIDTH - 1) + tap
        conv = conv + cw_ref[tap:tap + 1, :] * a[lo:lo + tm, :]
    act = (conv * jax.nn.sigmoid(conv) * u).astype(BF16)
    o_ref[...] += jnp.dot(act, wd_ref[...], preferred_element_type=F32)


def _conv_ffn(x2, ln_g, w_gate, w_up, conv_w, conv_b, w_down, layer, S, t):
    T, D = x2.shape
    F = w_gate.shape[2]
    tm, tf = t["ffn_tm"], t["ffn_tf"]
    halo_blocks = tm // HALO
    return pl.pallas_call(
        functools.partial(_ffn_kernel, tm=tm, tiles_per_seq=S // tm),
        grid=(T // tm, F // tf),
        in_specs=[
            pl.BlockSpec((tm, D), lambda i, f: (i, 0)),
            pl.BlockSpec((HALO, D), lambda i, f: (jnp.maximum(i * halo_blocks - 1, 0), 0)),
            pl.BlockSpec((1, D), lambda i, f: (0, 0)),
            pl.BlockSpec((None, D, tf), lambda i, f: (layer, 0, f)),
            pl.BlockSpec((None, D, tf), lambda i, f: (layer, 0, f)),
            pl.BlockSpec((CONV_WIDTH, tf), lambda i, f: (0, f)),
            pl.BlockSpec((1, tf), lambda i, f: (0, f)),
            pl.BlockSpec((None, tf, D), lambda i, f: (layer, f, 0)),
        ],
        out_specs=pl.BlockSpec((tm, D), lambda i, f: (i, 0)),
        out_shape=jax.ShapeDtypeStruct((T, D), F32),
        scratch_shapes=[pltpu.VMEM((HALO + tm, D), BF16)],
        compiler_params=_params(("arbitrary", "arbitrary"), t["vmem_limit"]),
    )(x2, x2, ln_g.reshape(1, D), w_gate, w_up, conv_w, conv_b.reshape(1, F), w_down)


def kernel(x, ln1_g, w_in, ret_norm_g, q_norm_g, k_norm_g, w_out, ln2_g, w_gate, w_up, conv_w, conv_b, w_down):
    B, S, D = x.shape
    depth = w_in.shape[0]
    t = _tiles()
    assert S % RET_CHUNK == 0 and S % MOBA_BLOCK == 0 and S % t["ffn_tm"] == 0
    assert w_in.shape[2] == 7 * GROUP

    log_decay = jnp.log(1.0 - 2.0 ** (-5.0 - jnp.arange(RET_HEADS, dtype=F32)))
    slopes = 2.0 ** (-8.0 * (jnp.arange(MOBA_HEADS, dtype=F32) + 1.0) / MOBA_HEADS)

    w_in, w_out, w_gate, w_up, w_down = (w.astype(BF16) for w in (w_in, w_out, w_gate, w_up, w_down))
    x2 = x.reshape(B * S, D)
    for l in range(depth):
        proj, gate = _in_proj(x2, ln1_g[l], w_in, l, q_norm_g[l], k_norm_g[l], t)
        y_r = _retention(proj, gate, ret_norm_g[l], log_decay, B, S, t)
        y_m = _moba(proj, slopes, B, S, t)
        x2 = _out_proj(y_r, y_m, w_out, l, x2, t)
        x2 = _conv_ffn(x2, ln2_g[l], w_gate, w_up, conv_w[l], conv_b[l], w_down, l, S, t)
    return x2.reshape(B, S, D)
```

```python
import functools

import jax
import jax.numpy as jnp
from jax import lax
from jax.experimental import pallas as pl
from jax.experimental.pallas import tpu as pltpu

RET_HEADS = 4
RET_DIM = 256
RET_CHUNK = 256
MOBA_HEADS = 8
MOBA_DIM = 128
MOBA_BLOCK = 256
MOBA_TOPK = 3
CONV_WIDTH = 3
NORM_EPS = 1e-6
NEG_INF = -1e30
LOG2E = 1.4426950408889634
GROUP = 1024

V7X_VMEM_BYTES = 64 * 1024 * 1024
V7X_BF16_SUBLANES = 16
V7X_MXU_COLS = 256

F32 = jnp.float32
BF16 = jnp.bfloat16


def _tiles():
    return dict(
        in_tm=512,
        out_tm=1024,
        ffn_tm=1024, ffn_tf=512,
        moba_heads_per_step=8,
        ret_ts=1024,
        vmem_limit=int(V7X_VMEM_BYTES * 0.92),
    )


def _params(semantics, vmem_limit):
    return pltpu.CompilerParams(dimension_semantics=semantics, vmem_limit_bytes=vmem_limit)


def _vmem_request(block_bytes):
    return min(block_bytes + V7X_VMEM_BYTES // 16, V7X_VMEM_BYTES - V7X_VMEM_BYTES // 32)


_GATE_GROUP = 3
_MQ_GROUP = 4
_MK_GROUP = 5


def _in_proj_kernel(x_ref, g_ref, w_ref, qg_ref, kg_ref, o_ref, gate_ref, h_ref):
    x = x_ref[...]
    ms = jnp.mean(x * x, axis=-1, keepdims=True)
    h_ref[...] = (x * lax.rsqrt(ms + NORM_EPS) * g_ref[...]).astype(BF16)

    cw = V7X_MXU_COLS
    gains = {_MQ_GROUP: qg_ref[...] * (MOBA_DIM ** -0.5 * LOG2E), _MK_GROUP: kg_ref[...]}
    for group in range(w_ref.shape[1] // GROUP):
        out_group = group - (group > _GATE_GROUP)
        for c in range(GROUP // cw):
            lo = c * cw
            acc = jnp.dot(h_ref[...], w_ref[:, group * GROUP + lo:group * GROUP + lo + cw],
                          preferred_element_type=F32)
            if group == _GATE_GROUP:
                gate_ref[:, lo:lo + cw] = acc
            elif group in gains:
                for hh in range(cw // MOBA_DIM):
                    seg = acc[:, hh * MOBA_DIM:(hh + 1) * MOBA_DIM]
                    ms = jnp.mean(seg * seg, axis=-1, keepdims=True)
                    at = out_group * GROUP + lo + hh * MOBA_DIM
                    o_ref[:, at:at + MOBA_DIM] = (seg * lax.rsqrt(ms + NORM_EPS) * gains[group]).astype(BF16)
            else:
                o_ref[:, out_group * GROUP + lo:out_group * GROUP + lo + cw] = acc.astype(BF16)


def _in_proj(x2, ln_g, w_in, layer, q_g, k_g, t):
    T, D = x2.shape
    n_cols = w_in.shape[2]
    tm = t["in_tm"]
    n_out = n_cols - GROUP
    vmem = D * n_cols * 2 + 2 * tm * D * 4 + 2 * tm * n_out * 2 + 2 * tm * GROUP * 4 + tm * D * 2
    return pl.pallas_call(
        _in_proj_kernel,
        grid=(T // tm,),
        in_specs=[
            pl.BlockSpec((tm, D), lambda i: (i, 0)),
            pl.BlockSpec((1, D), lambda i: (0, 0)),
            pl.BlockSpec((None, D, n_cols), lambda i: (layer, 0, 0), pipeline_mode=pl.Buffered(1)),
            pl.BlockSpec((1, MOBA_DIM), lambda i: (0, 0)),
            pl.BlockSpec((1, MOBA_DIM), lambda i: (0, 0)),
        ],
        out_specs=[
            pl.BlockSpec((tm, n_out), lambda i: (i, 0)),
            pl.BlockSpec((tm, GROUP), lambda i: (i, 0)),
        ],
        out_shape=[
            jax.ShapeDtypeStruct((T, n_out), BF16),
            jax.ShapeDtypeStruct((T, GROUP), F32),
        ],
        scratch_shapes=[pltpu.VMEM((tm, D), BF16)],
        compiler_params=_params(("arbitrary",), _vmem_request(vmem)),
    )(x2, ln_g.reshape(1, D), w_in, q_g.reshape(1, MOBA_DIM), k_g.reshape(1, MOBA_DIM))


def _retention_kernel(lg_ref, q_ref, k_ref, v_ref, gate_ref, g_ref, o_ref, state_ref, *, n_chunks):
    C = RET_CHUNK
    d = RET_DIM
    k_scale = d ** -0.5
    heads = range(RET_HEADS)
    cols = lambda h: slice(h * d, (h + 1) * d)
    nt = (((1,), (1,)), ((), ()))
    tn = (((0,), (0,)), ((), ()))

    @pl.when(pl.program_id(1) == 0)
    def _():
        state_ref[...] = jnp.zeros_like(state_ref)

    row = lax.broadcasted_iota(jnp.int32, (C, C), 0)
    col = lax.broadcasted_iota(jnp.int32, (C, C), 1)
    diff = (row - col).astype(F32)
    pos = lax.broadcasted_iota(jnp.int32, (C, 1), 0).astype(F32)
    inner_decay, q_decay, k_decay, chunk_decay = [], [], [], []
    for h in heads:
        lg = lg_ref[h]
        inner_decay.append(jnp.where(diff >= 0, jnp.exp(lg * jnp.maximum(diff, 0.0)), 0.0) * k_scale)
        q_decay.append(jnp.exp(lg * (pos + 1.0)))
        k_decay.append(jnp.exp(lg * (C - 1.0 - pos)) * k_scale)
        chunk_decay.append(jnp.exp(jnp.full((1, 1), lg * C, F32)))

    for n in range(n_chunks):
        sl = slice(n * C, (n + 1) * C)
        q = [q_ref[sl, cols(h)] for h in heads]
        k = [k_ref[sl, cols(h)] for h in heads]
        v = [v_ref[sl, cols(h)] for h in heads]
        state = [state_ref[h] for h in heads]
        raw = [lax.dot_general(q[h], k[h], nt, preferred_element_type=F32) for h in heads]
        cross = [jnp.dot(q[h], state[h].astype(BF16), preferred_element_type=F32) for h in heads]
        inner = [jnp.dot((raw[h] * inner_decay[h]).astype(BF16), v[h], preferred_element_type=F32) for h in heads]
        for h in heads:
            kd = (k[h].astype(F32) * k_decay[h]).astype(BF16)
            kv = lax.dot_general(kd, v[h], tn, preferred_element_type=F32)
            state_ref[h] = state[h] * chunk_decay[h] + kv
        for h in heads:
            y = inner[h] + cross[h] * q_decay[h]
            ms = jnp.mean(y * y, axis=-1, keepdims=True)
            y = y * lax.rsqrt(ms + NORM_EPS) * g_ref[h:h + 1, :]
            gt = gate_ref[sl, cols(h)]
            o_ref[sl, cols(h)] = (y * (gt * jax.nn.sigmoid(gt))).astype(BF16)


def _retention(proj, gate, ret_g, log_decay, B, S, t):
    T = B * S
    ts = t["ret_ts"]
    width = RET_HEADS * RET_DIM
    assert width == GROUP and S % ts == 0 and ts % RET_CHUNK == 0
    tiles = S // ts
    blk = lambda group: pl.BlockSpec((ts, width), lambda b, s: (b * tiles + s, group))
    return pl.pallas_call(
        functools.partial(_retention_kernel, n_chunks=ts // RET_CHUNK),
        grid=(B, tiles),
        in_specs=[
            pl.BlockSpec(memory_space=pltpu.SMEM),
            blk(0), blk(1), blk(2),
            pl.BlockSpec((ts, width), lambda b, s: (b * tiles + s, 0)),
            pl.BlockSpec((RET_HEADS, RET_DIM), lambda b, s: (0, 0)),
        ],
        out_specs=pl.BlockSpec((ts, width), lambda b, s: (b * tiles + s, 0)),
        out_shape=jax.ShapeDtypeStruct((T, width), BF16),
        scratch_shapes=[pltpu.VMEM((RET_HEADS, RET_DIM, RET_DIM), F32)],
        compiler_params=_params(("arbitrary", "arbitrary"), t["vmem_limit"]),
    )(log_decay, proj, proj, proj, gate, ret_g)


def _moba_kernel(slope_ref, q_ref, k_ref, v_ref, o_ref, kmh_ref, kml_ref, vt_ref, kx_ref, qa_ref,
                 *, n_blocks, heads):
    hg = pl.program_id(1)
    i = pl.program_id(2)
    L = MOBA_BLOCK
    dh = MOBA_DIM
    lanes = lambda g: slice(g * dh, (g + 1) * dh)
    CONST_ROWS = 8
    MASK_ROW0 = CONST_ROWS
    ONES_ROWS = vt_ref.shape[1] - dh

    @pl.when(i == 0)
    def _():
        lane = lax.broadcasted_iota(jnp.int32, (L, dh), 1)
        key = lax.broadcasted_iota(jnp.int32, (L, dh), 0)
        for n in range(n_blocks):
            kx = jnp.where(lane < 3, key, jnp.where(lane < 6, n * L, jnp.where(lane == MASK_ROW0 + n, 1, 0)))
            kx_ref[n * L:(n + 1) * L, :] = kx.astype(F32).astype(BF16)
        for g in range(heads):
            for n in range(n_blocks):
                rows = slice(n * L, (n + 1) * L)
                kb = k_ref[rows, lanes(g)].astype(F32)
                mean = jnp.sum(kb, axis=0, keepdims=True) * (1.0 / L)
                hi = mean.astype(BF16)
                kmh_ref[g, n:n + 1, :] = hi
                kml_ref[g, n:n + 1, :] = (mean - hi.astype(F32)).astype(BF16)
                vt_ref[g, :dh, rows] = v_ref[rows, lanes(g)].astype(F32).T.astype(BF16)
            vt_ref[g, dh:, :] = jnp.ones((ONES_ROWS, vt_ref.shape[2]), BF16)

    blk_id = lax.broadcasted_iota(jnp.int32, (n_blocks, L), 0)
    past = blk_id < i
    base = pl.multiple_of(i * L, L)
    part_row = lax.broadcasted_iota(jnp.int32, (CONST_ROWS, L), 0)
    causal = (lax.broadcasted_iota(jnp.int32, (L, L), 1) >= lax.broadcasted_iota(jnp.int32, (L, L), 0))

    qts = [q_ref[:, lanes(g)].astype(F32).T.astype(BF16) for g in range(heads)]
    gates = [jnp.dot(kmh_ref[g], qts[g], preferred_element_type=F32)
             + jnp.dot(kml_ref[g], qts[g], preferred_element_type=F32) for g in range(heads)]
    for g in range(heads):
        gate = jnp.where(past, gates[g], -jnp.inf)
        rank = jnp.zeros((n_blocks, L), jnp.int32)
        for m in range(n_blocks):
            gm = gate[m:m + 1, :]
            ahead = jnp.logical_or(gm > gate, jnp.logical_and(gm == gate, blk_id > m))
            rank = rank + ahead.astype(jnp.int32)
        mask_rows = jnp.where(jnp.logical_and(rank < MOBA_TOPK, past), 0.0, NEG_INF)
        c2 = jnp.full((CONST_ROWS, L), slope_ref[hg * heads + g] * LOG2E, F32)
        c_hi = c2.astype(BF16).astype(F32)
        c_mid = (c2 - c_hi).astype(BF16).astype(F32)
        c_lo = c2 - c_hi - c_mid
        sel3 = part_row % 3
        const_rows = jnp.where(part_row < 6, jnp.where(sel3 == 0, c_hi, jnp.where(sel3 == 1, c_mid, c_lo)), 0.0)
        extra = jnp.concatenate(
            [const_rows, mask_rows, jnp.zeros((dh - CONST_ROWS - n_blocks, L), F32)], axis=0).astype(BF16)
        qa_ref[g, :dh, :] = qts[g]
        qa_ref[g, dh:, :] = extra

    def scores(g, off, n_keys, kx):
        k_aug = jnp.concatenate([k_ref[pl.ds(off, n_keys), lanes(g)], kx], axis=1)
        return jnp.dot(k_aug, qa_ref[g], preferred_element_type=F32)

    def pair_scores(jj):
        off = pl.multiple_of(jj * (2 * L), 2 * L)
        kx = kx_ref[pl.ds(off, 2 * L), :]
        return [scores(g, off, 2 * L, kx) for g in range(heads)]

    kx_lane = lax.broadcasted_iota(jnp.int32, (L, dh), 1)
    kx_own = jnp.where(kx_lane < MASK_ROW0, kx_ref[pl.ds(base, L), :].astype(F32), 0.0).astype(BF16)
    own = [scores(g, base, L, kx_own) for g in range(heads)]
    init = []
    for g in range(heads):
        s = jnp.where(causal, own[g], NEG_INF)
        m0 = jnp.max(s, axis=0, keepdims=True)
        p = jnp.exp2(s - m0)
        acc0 = jnp.dot(vt_ref[g, :, pl.ds(base, L)], p.astype(BF16), preferred_element_type=F32)
        init += [m0, acc0]

    def past_pair(jj, carry):
        off = pl.multiple_of(jj * (2 * L), 2 * L)
        raw = pair_scores(jj)
        out = []
        for g in range(heads):
            m_prev, acc = carry[2 * g:2 * g + 2]
            s = raw[g]
            m_new = jnp.maximum(m_prev, jnp.max(s, axis=0, keepdims=True))
            alpha = jnp.exp2(m_prev - m_new)
            p = jnp.exp2(s - m_new)
            pv = jnp.dot(vt_ref[g, :, pl.ds(off, 2 * L)], p.astype(BF16), preferred_element_type=F32)
            out += [m_new, alpha * acc + pv]
        return tuple(out)

    fin = lax.fori_loop(0, (i + 1) // 2, past_pair, tuple(init))
    for g in range(heads):
        acc = fin[2 * g + 1]
        o_ref[:, lanes(g)] = (acc[:dh] / acc[dh:dh + 1]).T.astype(BF16)


def _moba(proj, slopes, B, S, t):
    T = B * S
    nb = S // MOBA_BLOCK
    G = t["moba_heads_per_step"]
    per = GROUP // (G * MOBA_DIM)
    q_off, k_off, v_off = 3 * per, 4 * per, 5 * per
    return pl.pallas_call(
        functools.partial(_moba_kernel, n_blocks=nb, heads=G),
        grid=(B, MOBA_HEADS // G, nb),
        in_specs=[
            pl.BlockSpec(memory_space=pltpu.SMEM),
            pl.BlockSpec((MOBA_BLOCK, G * MOBA_DIM), lambda b, h, i: (b * nb + i, q_off + h)),
            pl.BlockSpec((S, G * MOBA_DIM), lambda b, h, i: (b, k_off + h)),
            pl.BlockSpec((S, G * MOBA_DIM), lambda b, h, i: (b, v_off + h)),
        ],
        out_specs=pl.BlockSpec((MOBA_BLOCK, G * MOBA_DIM), lambda b, h, i: (b * nb + i, h)),
        out_shape=jax.ShapeDtypeStruct((T, MOBA_HEADS * MOBA_DIM), BF16),
        scratch_shapes=[
            pltpu.VMEM((G, nb, MOBA_DIM), BF16),
            pltpu.VMEM((G, nb, MOBA_DIM), BF16),
            pltpu.VMEM((G, MOBA_DIM + V7X_BF16_SUBLANES, S), BF16),
            pltpu.VMEM((S, MOBA_DIM), BF16),
            pltpu.VMEM((G, 2 * MOBA_DIM, MOBA_BLOCK), BF16),
        ],
        compiler_params=_params(("arbitrary", "arbitrary", "arbitrary"), t["vmem_limit"]),
    )(slopes, proj, proj, proj)


def _out_proj_kernel(yr_ref, ym_ref, w_ref, x_ref, o_ref):
    kr = yr_ref.shape[1]
    cw = V7X_MXU_COLS
    for c in range(o_ref.shape[1] // cw):
        cols = slice(c * cw, (c + 1) * cw)
        o_ref[:, cols] = (x_ref[:, cols]
                          + jnp.dot(yr_ref[...], w_ref[:kr, cols], preferred_element_type=F32)
                          + jnp.dot(ym_ref[...], w_ref[kr:, cols], preferred_element_type=F32))


def _out_proj(y_r, y_m, w_out, layer, x2, t):
    T, D = x2.shape
    tm = t["out_tm"]
    kr, km = y_r.shape[1], y_m.shape[1]
    vmem = (kr + km) * D * 2 + 2 * tm * (kr + km) * 2 + 4 * tm * D * 4
    return pl.pallas_call(
        _out_proj_kernel,
        grid=(T // tm,),
        in_specs=[
            pl.BlockSpec((tm, kr), lambda i: (i, 0)),
            pl.BlockSpec((tm, km), lambda i: (i, 0)),
            pl.BlockSpec((None, kr + km, D), lambda i: (layer, 0, 0), pipeline_mode=pl.Buffered(1)),
            pl.BlockSpec((tm, D), lambda i: (i, 0)),
        ],
        out_specs=pl.BlockSpec((tm, D), lambda i: (i, 0)),
        out_shape=jax.ShapeDtypeStruct((T, D), F32),
        compiler_params=_params(("arbitrary",), _vmem_request(vmem)),
    )(y_r, y_m, w_out, x2)


HALO = V7X_BF16_SUBLANES


def _ffn_kernel(x_ref, xh_ref, g_ref, wg_ref, wu_ref, cw_ref, cb_ref, wd_ref, o_ref, h_ref, *, tm, tiles_per_seq):
    i = pl.program_id(0)
    f = pl.program_id(1)

    def norm(x):
        ms = jnp.mean(x * x, axis=-1, keepdims=True)
        return (x * lax.rsqrt(ms + NORM_EPS) * g_ref[...]).astype(BF16)

    @pl.when(f == 0)
    def _():
        x = x_ref[...]
        h_ref[HALO:, :] = norm(x)
        seq_start = (i % tiles_per_seq) == 0
        h_ref[:HALO, :] = jnp.where(seq_start, jnp.zeros((HALO, x.shape[1]), BF16), norm(xh_ref[...]))
        o_ref[...] = x

    a = jnp.dot(h_ref[...], wg_ref[...], preferred_element_type=F32)
    u = jnp.dot(h_ref[HALO:, :], wu_ref[...], preferred_element_type=F32)
    conv = cb_ref[...]
    for tap in range(CONV_WIDTH):
        lo = HALO - (CONV_WIDTH - 1) + tap
        conv = conv + cw_ref[tap:tap + 1, :] * a[lo:lo + tm, :]
    act = (conv * jax.nn.sigmoid(conv) * u).astype(BF16)
    o_ref[...] += jnp.dot(act, wd_ref[...], preferred_element_type=F32)


def _conv_ffn(x2, ln_g, w_gate, w_up, conv_w, conv_b, w_down, layer, S, t):
    T, D = x2.shape
    F = w_gate.shape[2]
    tm, tf = t["ffn_tm"], t["ffn_tf"]
    halo_blocks = tm // HALO
    return pl.pallas_call(
        functools.partial(_ffn_kernel, tm=tm, tiles_per_seq=S // tm),
        grid=(T // tm, F // tf),
        in_specs=[
            pl.BlockSpec((tm, D), lambda i, f: (i, 0)),
            pl.BlockSpec((HALO, D), lambda i, f: (jnp.maximum(i * halo_blocks - 1, 0), 0)),
            pl.BlockSpec((1, D), lambda i, f: (0, 0)),
            pl.BlockSpec((None, D, tf), lambda i, f: (layer, 0, f)),
            pl.BlockSpec((None, D, tf), lambda i, f: (layer, 0, f)),
            pl.BlockSpec((CONV_WIDTH, tf), lambda i, f: (0, f)),
            pl.BlockSpec((1, tf), lambda i, f: (0, f)),
            pl.BlockSpec((None, tf, D), lambda i, f: (layer, f, 0)),
        ],
        out_specs=pl.BlockSpec((tm, D), lambda i, f: (i, 0)),
        out_shape=jax.ShapeDtypeStruct((T, D), F32),
        scratch_shapes=[pltpu.VMEM((HALO + tm, D), BF16)],
        compiler_params=_params(("arbitrary", "arbitrary"), t["vmem_limit"]),
    )(x2, x2, ln_g.reshape(1, D), w_gate, w_up, conv_w, conv_b.reshape(1, F), w_down)


def kernel(x, ln1_g, w_in, ret_norm_g, q_norm_g, k_norm_g, w_out, ln2_g, w_gate, w_up, conv_w, conv_b, w_down):
    B, S, D = x.shape
    depth = w_in.shape[0]
    t = _tiles()
    assert S % RET_CHUNK == 0 and S % MOBA_BLOCK == 0 and S % t["ffn_tm"] == 0
    assert w_in.shape[2] == 7 * GROUP

    log_decay = jnp.log(1.0 - 2.0 ** (-5.0 - jnp.arange(RET_HEADS, dtype=F32)))
    slopes = 2.0 ** (-8.0 * (jnp.arange(MOBA_HEADS, dtype=F32) + 1.0) / MOBA_HEADS)

    w_in, w_out, w_gate, w_up, w_down = (w.astype(BF16) for w in (w_in, w_out, w_gate, w_up, w_down))
    x2 = x.reshape(B * S, D)
    for l in range(depth):
        proj, gate = _in_proj(x2, ln1_g[l], w_in, l, q_norm_g[l], k_norm_g[l], t)
        y_r = _retention(proj, gate, ret_norm_g[l], log_decay, B, S, t)
        y_m = _moba(proj, slopes, B, S, t)
        x2 = _out_proj(y_r, y_m, w_out, l, x2, t)
        x2 = _conv_ffn(x2, ln2_g[l], w_gate, w_up, conv_w[l], conv_b[l], w_down, l, S, t)
    return x2.reshape(B, S, D)
```

```python
import functools

import jax
import jax.numpy as jnp
from jax import lax
from jax.experimental import pallas as pl
from jax.experimental.pallas import tpu as pltpu

RET_HEADS = 4
RET_DIM = 256
RET_CHUNK = 256
MOBA_HEADS = 8
MOBA_DIM = 128
MOBA_BLOCK = 256
MOBA_TOPK = 3
CONV_WIDTH = 3
NORM_EPS = 1e-6
NEG_INF = -1e30
LOG2E = 1.4426950408889634
GROUP = 1024

V7X_VMEM_BYTES = 64 * 1024 * 1024
V7X_BF16_SUBLANES = 16
V7X_MXU_COLS = 256

F32 = jnp.float32
BF16 = jnp.bfloat16


def _tiles():
    return dict(
        in_tm=512,
        out_tm=1024,
        ffn_tm=1024, ffn_tf=512,
        moba_heads_per_step=8,
        ret_ts=1024,
        vmem_limit=int(V7X_VMEM_BYTES * 0.92),
    )


def _params(semantics, vmem_limit):
    return pltpu.CompilerParams(dimension_semantics=semantics, vmem_limit_bytes=vmem_limit)


def _vmem_request(block_bytes):
    return min(block_bytes + V7X_VMEM_BYTES // 16, V7X_VMEM_BYTES - V7X_VMEM_BYTES // 32)


_GATE_GROUP = 3
_MQ_GROUP = 4
_MK_GROUP = 5


def _in_proj_kernel(x_ref, g_ref, w_ref, qg_ref, kg_ref, o_ref, gate_ref, h_ref):
    x = x_ref[...]
    ms = jnp.mean(x * x, axis=-1, keepdims=True)
    h_ref[...] = (x * lax.rsqrt(ms + NORM_EPS) * g_ref[...]).astype(BF16)

    cw = V7X_MXU_COLS
    gains = {_MQ_GROUP: qg_ref[...] * (MOBA_DIM ** -0.5 * LOG2E), _MK_GROUP: kg_ref[...]}
    for group in range(w_ref.shape[1] // GROUP):
        out_group = group - (group > _GATE_GROUP)
        for c in range(GROUP // cw):
            lo = c * cw
            acc = jnp.dot(h_ref[...], w_ref[:, group * GROUP + lo:group * GROUP + lo + cw],
                          preferred_element_type=F32)
            if group == _GATE_GROUP:
                gate_ref[:, lo:lo + cw] = acc
            elif group in gains:
                for hh in range(cw // MOBA_DIM):
                    seg = acc[:, hh * MOBA_DIM:(hh + 1) * MOBA_DIM]
                    ms = jnp.mean(seg * seg, axis=-1, keepdims=True)
                    at = out_group * GROUP + lo + hh * MOBA_DIM
                    o_ref[:, at:at + MOBA_DIM] = (seg * lax.rsqrt(ms + NORM_EPS) * gains[group]).astype(BF16)
            else:
                o_ref[:, out_group * GROUP + lo:out_group * GROUP + lo + cw] = acc.astype(BF16)


def _in_proj(x2, ln_g, w_in, layer, q_g, k_g, t):
    T, D = x2.shape
    n_cols = w_in.shape[2]
    tm = t["in_tm"]
    n_out = n_cols - GROUP
    vmem = D * n_cols * 2 + 2 * tm * D * 4 + 2 * tm * n_out * 2 + 2 * tm * GROUP * 4 + tm * D * 2
    return pl.pallas_call(
        _in_proj_kernel,
        grid=(T // tm,),
        in_specs=[
            pl.BlockSpec((tm, D), lambda i: (i, 0)),
            pl.BlockSpec((1, D), lambda i: (0, 0)),
            pl.BlockSpec((None, D, n_cols), lambda i: (layer, 0, 0), pipeline_mode=pl.Buffered(1)),
            pl.BlockSpec((1, MOBA_DIM), lambda i: (0, 0)),
            pl.BlockSpec((1, MOBA_DIM), lambda i: (0, 0)),
        ],
        out_specs=[
            pl.BlockSpec((tm, n_out), lambda i: (i, 0)),
            pl.BlockSpec((tm, GROUP), lambda i: (i, 0)),
        ],
        out_shape=[
            jax.ShapeDtypeStruct((T, n_out), BF16),
            jax.ShapeDtypeStruct((T, GROUP), F32),
        ],
        scratch_shapes=[pltpu.VMEM((tm, D), BF16)],
        compiler_params=_params(("arbitrary",), _vmem_request(vmem)),
    )(x2, ln_g.reshape(1, D), w_in, q_g.reshape(1, MOBA_DIM), k_g.reshape(1, MOBA_DIM))


def _retention_kernel(lg_ref, q_ref, k_ref, v_ref, gate_ref, g_ref, o_ref, state_ref, *, n_chunks):
    C = RET_CHUNK
    d = RET_DIM
    k_scale = d ** -0.5
    heads = range(RET_HEADS)
    cols = lambda h: slice(h * d, (h + 1) * d)
    nt = (((1,), (1,)), ((), ()))
    tn = (((0,), (0,)), ((), ()))

    @pl.when(pl.program_id(1) == 0)
    def _():
        state_ref[...] = jnp.zeros_like(state_ref)

    row = lax.broadcasted_iota(jnp.int32, (C, C), 0)
    col = lax.broadcasted_iota(jnp.int32, (C, C), 1)
    diff = (row - col).astype(F32)
    pos = lax.broadcasted_iota(jnp.int32, (C, 1), 0).astype(F32)
    inner_decay, q_decay, k_decay, chunk_decay = [], [], [], []
    for h in heads:
        lg = lg_ref[h]
        inner_decay.append(jnp.where(diff >= 0, jnp.exp(lg * jnp.maximum(diff, 0.0)), 0.0) * k_scale)
        q_decay.append(jnp.exp(lg * (pos + 1.0)))
        k_decay.append(jnp.exp(lg * (C - 1.0 - pos)) * k_scale)
        chunk_decay.append(jnp.exp(jnp.full((1, 1), lg * C, F32)))

    for n in range(n_chunks):
        sl = slice(n * C, (n + 1) * C)
        q = [q_ref[sl, cols(h)] for h in heads]
        k = [k_ref[sl, cols(h)] for h in heads]
        v = [v_ref[sl, cols(h)] for h in heads]
        state = [state_ref[h] for h in heads]
        raw = [lax.dot_general(q[h], k[h], nt, preferred_element_type=F32) for h in heads]
        cross = [jnp.dot(q[h], state[h].astype(BF16), preferred_element_type=F32) for h in heads]
        inner = [jnp.dot((raw[h] * inner_decay[h]).astype(BF16), v[h], preferred_element_type=F32) for h in heads]
        for h in heads:
            kd = (k[h].astype(F32) * k_decay[h]).astype(BF16)
            kv = lax.dot_general(kd, v[h], tn, preferred_element_type=F32)
            state_ref[h] = state[h] * chunk_decay[h] + kv
        for h in heads:
            y = inner[h] + cross[h] * q_decay[h]
            ms = jnp.mean(y * y, axis=-1, keepdims=True)
            y = y * lax.rsqrt(ms + NORM_EPS) * g_ref[h:h + 1, :]
            gt = gate_ref[sl, cols(h)]
            o_ref[sl, cols(h)] = (y * (gt * jax.nn.sigmoid(gt))).astype(BF16)


def _retention(proj, gate, ret_g, log_decay, B, S, t):
    T = B * S
    ts = t["ret_ts"]
    width = RET_HEADS * RET_DIM
    assert width == GROUP and S % ts == 0 and ts % RET_CHUNK == 0
    tiles = S // ts
    blk = lambda group: pl.BlockSpec((ts, width), lambda b, s: (b * tiles + s, group))
    return pl.pallas_call(
        functools.partial(_retention_kernel, n_chunks=ts // RET_CHUNK),
        grid=(B, tiles),
        in_specs=[
            pl.BlockSpec(memory_space=pltpu.SMEM),
            blk(0), blk(1), blk(2),
            pl.BlockSpec((ts, width), lambda b, s: (b * tiles + s, 0)),
            pl.BlockSpec((RET_HEADS, RET_DIM), lambda b, s: (0, 0)),
        ],
        out_specs=pl.BlockSpec((ts, width), lambda b, s: (b * tiles + s, 0)),
        out_shape=jax.ShapeDtypeStruct((T, width), BF16),
        scratch_shapes=[pltpu.VMEM((RET_HEADS, RET_DIM, RET_DIM), F32)],
        compiler_params=_params(("arbitrary", "arbitrary"), t["vmem_limit"]),
    )(log_decay, proj, proj, proj, gate, ret_g)


def _moba_kernel(slope_ref, q_ref, k_ref, v_ref, o_ref, kmh_ref, kml_ref, vt_ref, kx_ref, qa_ref,
                 *, n_blocks, heads):
    hg = pl.program_id(1)
    i = pl.program_id(2)
    L = MOBA_BLOCK
    dh = MOBA_DIM
    lanes = lambda g: slice(g * dh, (g + 1) * dh)
    CONST_ROWS = 8
    MASK_ROW0 = CONST_ROWS
    ONES_ROWS = vt_ref.shape[1] - dh

    @pl.when(i == 0)
    def _():
        lane = lax.broadcasted_iota(jnp.int32, (L, dh), 1)
        key = lax.broadcasted_iota(jnp.int32, (L, dh), 0)
        for n in range(n_blocks):
            kx = jnp.where(lane < 3, key, jnp.where(lane < 6, n * L, jnp.where(lane == MASK_ROW0 + n, 1, 0)))
            kx_ref[n * L:(n + 1) * L, :] = kx.astype(F32).astype(BF16)
        for g in range(heads):
            for n in range(n_blocks):
                rows = slice(n * L, (n + 1) * L)
                kb = k_ref[rows, lanes(g)].astype(F32)
                mean = jnp.sum(kb, axis=0, keepdims=True) * (1.0 / L)
                hi = mean.astype(BF16)
                kmh_ref[g, n:n + 1, :] = hi
                kml_ref[g, n:n + 1, :] = (mean - hi.astype(F32)).astype(BF16)
                vt_ref[g, :dh, rows] = v_ref[rows, lanes(g)].astype(F32).T.astype(BF16)
            vt_ref[g, dh:, :] = jnp.ones((ONES_ROWS, vt_ref.shape[2]), BF16)

    blk_id = lax.broadcasted_iota(jnp.int32, (n_blocks, L), 0)
    blk_f = blk_id.astype(F32)
    past = blk_id < i
    base = pl.multiple_of(i * L, L)
    part_row = lax.broadcasted_iota(jnp.int32, (CONST_ROWS, L), 0)
    causal = (lax.broadcasted_iota(jnp.int32, (L, L), 1) >= lax.broadcasted_iota(jnp.int32, (L, L), 0))

    qts = [q_ref[:, lanes(g)].astype(F32).T.astype(BF16) for g in range(heads)]
    gates = [jnp.dot(kmh_ref[g], qts[g], preferred_element_type=F32)
             + jnp.dot(kml_ref[g], qts[g], preferred_element_type=F32) for g in range(heads)]
    for g in range(heads):
        gate = jnp.where(past, gates[g], -jnp.inf)
        chosen = jnp.zeros((n_blocks, L), jnp.bool_)
        for _ in range(MOBA_TOPK):
            best = jnp.max(gate, axis=0, keepdims=True)
            first = jnp.min(jnp.where(gate == best, blk_f, float(n_blocks)), axis=0, keepdims=True)
            hit = blk_f == first
            chosen = jnp.logical_or(chosen, hit)
            gate = jnp.where(hit, -jnp.inf, gate)
        mask_rows = jnp.where(jnp.logical_and(chosen, past), 0.0, NEG_INF)
        c2 = jnp.full((CONST_ROWS, L), slope_ref[hg * heads + g] * LOG2E, F32)
        c_hi = c2.astype(BF16).astype(F32)
        c_mid = (c2 - c_hi).astype(BF16).astype(F32)
        c_lo = c2 - c_hi - c_mid
        sel3 = part_row % 3
        const_rows = jnp.where(part_row < 6, jnp.where(sel3 == 0, c_hi, jnp.where(sel3 == 1, c_mid, c_lo)), 0.0)
        extra = jnp.concatenate(
            [const_rows, mask_rows, jnp.zeros((dh - CONST_ROWS - n_blocks, L), F32)], axis=0).astype(BF16)
        qa_ref[g, :dh, :] = qts[g]
        qa_ref[g, dh:, :] = extra

    def scores(g, off, n_keys, kx):
        k_aug = jnp.concatenate([k_ref[pl.ds(off, n_keys), lanes(g)], kx], axis=1)
        return jnp.dot(k_aug, qa_ref[g], preferred_element_type=F32)

    def pair_scores(jj):
        off = pl.multiple_of(jj * (2 * L), 2 * L)
        kx = kx_ref[pl.ds(off, 2 * L), :]
        return [scores(g, off, 2 * L, kx) for g in range(heads)]

    kx_lane = lax.broadcasted_iota(jnp.int32, (L, dh), 1)
    kx_own = jnp.where(kx_lane < MASK_ROW0, kx_ref[pl.ds(base, L), :].astype(F32), 0.0).astype(BF16)
    own = [scores(g, base, L, kx_own) for g in range(heads)]
    first = pair_scores(0)
    init = []
    for g in range(heads):
        s = jnp.where(causal, own[g], NEG_INF)
        m0 = jnp.max(s, axis=0, keepdims=True)
        p = jnp.exp2(s - m0)
        acc0 = jnp.dot(vt_ref[g, :, pl.ds(base, L)], p.astype(BF16), preferred_element_type=F32)
        init += [m0, acc0]

    def fold_pair(jj, raw, carry):
        off = pl.multiple_of(jj * (2 * L), 2 * L)
        out = []
        for g in range(heads):
            m_prev, acc = carry[2 * g:2 * g + 2]
            s = raw[g]
            m_new = jnp.maximum(m_prev, jnp.max(s, axis=0, keepdims=True))
            alpha = jnp.exp2(m_prev - m_new)
            p = jnp.exp2(s - m_new)
            pv = jnp.dot(vt_ref[g, :, pl.ds(off, 2 * L)], p.astype(BF16), preferred_element_type=F32)
            out += [m_new, alpha * acc + pv]
        return tuple(out)

    def past_pair(jj, carry):
        return fold_pair(jj, pair_scores(jj), carry)

    fin = lax.fori_loop(1, (i + 1) // 2, past_pair, fold_pair(0, first, tuple(init)))
    for g in range(heads):
        acc = fin[2 * g + 1]
        o_ref[:, lanes(g)] = (acc[:dh] / acc[dh:dh + 1]).T.astype(BF16)


def _moba(proj, slopes, B, S, t):
    T = B * S
    nb = S // MOBA_BLOCK
    G = t["moba_heads_per_step"]
    per = GROUP // (G * MOBA_DIM)
    q_off, k_off, v_off = 3 * per, 4 * per, 5 * per
    return pl.pallas_call(
        functools.partial(_moba_kernel, n_blocks=nb, heads=G),
        grid=(B, MOBA_HEADS // G, nb),
        in_specs=[
            pl.BlockSpec(memory_space=pltpu.SMEM),
            pl.BlockSpec((MOBA_BLOCK, G * MOBA_DIM), lambda b, h, i: (b * nb + i, q_off + h)),
            pl.BlockSpec((S, G * MOBA_DIM), lambda b, h, i: (b, k_off + h)),
            pl.BlockSpec((S, G * MOBA_DIM), lambda b, h, i: (b, v_off + h)),
        ],
        out_specs=pl.BlockSpec((MOBA_BLOCK, G * MOBA_DIM), lambda b, h, i: (b * nb + i, h)),
        out_shape=jax.ShapeDtypeStruct((T, MOBA_HEADS * MOBA_DIM), BF16),
        scratch_shapes=[
            pltpu.VMEM((G, nb, MOBA_DIM), BF16),
            pltpu.VMEM((G, nb, MOBA_DIM), BF16),
            pltpu.VMEM((G, MOBA_DIM + V7X_BF16_SUBLANES, S), BF16),
            pltpu.VMEM((S, MOBA_DIM), BF16),
            pltpu.VMEM((G, 2 * MOBA_DIM, MOBA_BLOCK), BF16),
        ],
        compiler_params=_params(("arbitrary", "arbitrary", "arbitrary"), t["vmem_limit"]),
    )(slopes, proj, proj, proj)


def _out_proj_kernel(yr_ref, ym_ref, w_ref, x_ref, o_ref):
    kr = yr_ref.shape[1]
    cw = V7X_MXU_COLS
    for c in range(o_ref.shape[1] // cw):
        cols = slice(c * cw, (c + 1) * cw)
        o_ref[:, cols] = (x_ref[:, cols]
                          + jnp.dot(yr_ref[...], w_ref[:kr, cols], preferred_element_type=F32)
                          + jnp.dot(ym_ref[...], w_ref[kr:, cols], preferred_element_type=F32))


def _out_proj(y_r, y_m, w_out, layer, x2, t):
    T, D = x2.shape
    tm = t["out_tm"]
    kr, km = y_r.shape[1], y_m.shape[1]
    vmem = (kr + km) * D * 2 + 2 * tm * (kr + km) * 2 + 4 * tm * D * 4
    return pl.pallas_call(
        _out_proj_kernel,
        grid=(T // tm,),
        in_specs=[
            pl.BlockSpec((tm, kr), lambda i: (i, 0)),
            pl.BlockSpec((tm, km), lambda i: (i, 0)),
            pl.BlockSpec((None, kr + km, D), lambda i: (layer, 0, 0), pipeline_mode=pl.Buffered(1)),
            pl.BlockSpec((tm, D), lambda i: (i, 0)),
        ],
        out_specs=pl.BlockSpec((tm, D), lambda i: (i, 0)),
        out_shape=jax.ShapeDtypeStruct((T, D), F32),
        compiler_params=_params(("arbitrary",), _vmem_request(vmem)),
    )(y_r, y_m, w_out, x2)


HALO = V7X_BF16_SUBLANES


def _ffn_kernel(x_ref, xh_ref, g_ref, wg_ref, wu_ref, cw_ref, cb_ref, wd_ref, o_ref, h_ref, *, tm, tiles_per_seq):
    i = pl.program_id(0)
    f = pl.program_id(1)

    def norm(x):
        ms = jnp.mean(x * x, axis=-1, keepdims=True)
        return (x * lax.rsqrt(ms + NORM_EPS) * g_ref[...]).astype(BF16)

    @pl.when(f == 0)
    def _():
        x = x_ref[...]
        h_ref[HALO:, :] = norm(x)
        seq_start = (i % tiles_per_seq) == 0
        h_ref[:HALO, :] = jnp.where(seq_start, jnp.zeros((HALO, x.shape[1]), BF16), norm(xh_ref[...]))
        o_ref[...] = x

    a = jnp.dot(h_ref[...], wg_ref[...], preferred_element_type=F32)
    u = jnp.dot(h_ref[HALO:, :], wu_ref[...], preferred_element_type=F32)
    conv = cb_ref[...]
    for tap in range(CONV_WIDTH):
        lo = HALO - (CONV_WIDTH - 1) + tap
        conv = conv + cw_ref[tap:tap + 1, :] * a[lo:lo + tm, :]
    act = (conv * jax.nn.sigmoid(conv) * u).astype(BF16)
    o_ref[...] += jnp.dot(act, wd_ref[...], preferred_element_type=F32)


def _conv_ffn(x2, ln_g, w_gate, w_up, conv_w, conv_b, w_down, layer, S, t):
    T, D = x2.shape
    F = w_gate.shape[2]
    tm, tf = t["ffn_tm"], t["ffn_tf"]
    halo_blocks = tm // HALO
    return pl.pallas_call(
        functools.partial(_ffn_kernel, tm=tm, tiles_per_seq=S // tm),
        grid=(T // tm, F // tf),
        in_specs=[
            pl.BlockSpec((tm, D), lambda i, f: (i, 0)),
            pl.BlockSpec((HALO, D), lambda i, f: (jnp.maximum(i * halo_blocks - 1, 0), 0)),
            pl.BlockSpec((1, D), lambda i, f: (0, 0)),
            pl.BlockSpec((None, D, tf), lambda i, f: (layer, 0, f)),
            pl.BlockSpec((None, D, tf), lambda i, f: (layer, 0, f)),
            pl.BlockSpec((CONV_WIDTH, tf), lambda i, f: (0, f)),
            pl.BlockSpec((1, tf), lambda i, f: (0, f)),
            pl.BlockSpec((None, tf, D), lambda i, f: (layer, f, 0)),
        ],
        out_specs=pl.BlockSpec((tm, D), lambda i, f: (i, 0)),
        out_shape=jax.ShapeDtypeStruct((T, D), F32),
        scratch_shapes=[pltpu.VMEM((HALO + tm, D), BF16)],
        compiler_params=_params(("arbitrary", "arbitrary"), t["vmem_limit"]),
    )(x2, x2, ln_g.reshape(1, D), w_gate, w_up, conv_w, conv_b.reshape(1, F), w_down)


def kernel(x, ln1_g, w_in, ret_norm_g, q_norm_g, k_norm_g, w_out, ln2_g, w_gate, w_up, conv_w, conv_b, w_down):
    B, S, D = x.shape
    depth = w_in.shape[0]
    t = _tiles()
    assert S % RET_CHUNK == 0 and S % MOBA_BLOCK == 0 and S % t["ffn_tm"] == 0
    assert w_in.shape[2] == 7 * GROUP

    log_decay = jnp.log(1.0 - 2.0 ** (-5.0 - jnp.arange(RET_HEADS, dtype=F32)))
    slopes = 2.0 ** (-8.0 * (jnp.arange(MOBA_HEADS, dtype=F32) + 1.0) / MOBA_HEADS)

    w_in, w_out, w_gate, w_up, w_down = (w.astype(BF16) for w in (w_in, w_out, w_gate, w_up, w_down))
    x2 = x.reshape(B * S, D)
    for l in range(depth):
        proj, gate = _in_proj(x2, ln1_g[l], w_in, l, q_norm_g[l], k_norm_g[l], t)
        y_r = _retention(proj, gate, ret_norm_g[l], log_decay, B, S, t)
        y_m = _moba(proj, slopes, B, S, t)
        x2 = _out_proj(y_r, y_m, w_out, l, x2, t)
        x2 = _conv_ffn(x2, ln2_g[l], w_gate, w_up, conv_w[l], conv_b[l], w_down, l, S, t)
    return x2.reshape(B, S, D)
```

```python
import functools

import jax
import jax.numpy as jnp
from jax import lax
from jax.experimental import pallas as pl
from jax.experimental.pallas import tpu as pltpu

RET_HEADS = 4
RET_DIM = 256
RET_CHUNK = 256
MOBA_HEADS = 8
MOBA_DIM = 128
MOBA_BLOCK = 256
MOBA_TOPK = 3
CONV_WIDTH = 3
NORM_EPS = 1e-6
NEG_INF = -1e30
LOG2E = 1.4426950408889634
GROUP = 1024

V7X_VMEM_BYTES = 64 * 1024 * 1024
V7X_BF16_SUBLANES = 16
V7X_MXU_COLS = 256

F32 = jnp.float32
BF16 = jnp.bfloat16


def _tiles():
    return dict(
        in_tm=512,
        out_tm=1024,
        ffn_tm=1024, ffn_tf=512,
        moba_heads_per_step=8,
        moba_query_blocks_per_step=2,
        ret_ts=1024,
        vmem_limit=int(V7X_VMEM_BYTES * 0.92),
    )


def _params(semantics, vmem_limit):
    return pltpu.CompilerParams(dimension_semantics=semantics, vmem_limit_bytes=vmem_limit)


def _vmem_request(block_bytes):
    return min(block_bytes + V7X_VMEM_BYTES // 16, V7X_VMEM_BYTES - V7X_VMEM_BYTES // 32)


_GATE_GROUP = 3
_MQ_GROUP = 4
_MK_GROUP = 5


def _in_proj_kernel(x_ref, g_ref, w_ref, qg_ref, kg_ref, o_ref, gate_ref, h_ref):
    x = x_ref[...]
    ms = jnp.mean(x * x, axis=-1, keepdims=True)
    h_ref[...] = (x * lax.rsqrt(ms + NORM_EPS) * g_ref[...]).astype(BF16)

    cw = V7X_MXU_COLS
    gains = {_MQ_GROUP: qg_ref[...] * (MOBA_DIM ** -0.5 * LOG2E), _MK_GROUP: kg_ref[...]}
    for group in range(w_ref.shape[1] // GROUP):
        out_group = group - (group > _GATE_GROUP)
        for c in range(GROUP // cw):
            lo = c * cw
            acc = jnp.dot(h_ref[...], w_ref[:, group * GROUP + lo:group * GROUP + lo + cw],
                          preferred_element_type=F32)
            if group == _GATE_GROUP:
                gate_ref[:, lo:lo + cw] = acc
            elif group in gains:
                for hh in range(cw // MOBA_DIM):
                    seg = acc[:, hh * MOBA_DIM:(hh + 1) * MOBA_DIM]
                    ms = jnp.mean(seg * seg, axis=-1, keepdims=True)
                    at = out_group * GROUP + lo + hh * MOBA_DIM
                    o_ref[:, at:at + MOBA_DIM] = (seg * lax.rsqrt(ms + NORM_EPS) * gains[group]).astype(BF16)
            else:
                o_ref[:, out_group * GROUP + lo:out_group * GROUP + lo + cw] = acc.astype(BF16)


def _in_proj(x2, ln_g, w_in, layer, q_g, k_g, t):
    T, D = x2.shape
    n_cols = w_in.shape[2]
    tm = t["in_tm"]
    n_out = n_cols - GROUP
    vmem = D * n_cols * 2 + 2 * tm * D * 4 + 2 * tm * n_out * 2 + 2 * tm * GROUP * 4 + tm * D * 2
    return pl.pallas_call(
        _in_proj_kernel,
        grid=(T // tm,),
        in_specs=[
            pl.BlockSpec((tm, D), lambda i: (i, 0)),
            pl.BlockSpec((1, D), lambda i: (0, 0)),
            pl.BlockSpec((None, D, n_cols), lambda i: (layer, 0, 0), pipeline_mode=pl.Buffered(1)),
            pl.BlockSpec((1, MOBA_DIM), lambda i: (0, 0)),
            pl.BlockSpec((1, MOBA_DIM), lambda i: (0, 0)),
        ],
        out_specs=[
            pl.BlockSpec((tm, n_out), lambda i: (i, 0)),
            pl.BlockSpec((tm, GROUP), lambda i: (i, 0)),
        ],
        out_shape=[
            jax.ShapeDtypeStruct((T, n_out), BF16),
            jax.ShapeDtypeStruct((T, GROUP), F32),
        ],
        scratch_shapes=[pltpu.VMEM((tm, D), BF16)],
        compiler_params=_params(("arbitrary",), _vmem_request(vmem)),
    )(x2, ln_g.reshape(1, D), w_in, q_g.reshape(1, MOBA_DIM), k_g.reshape(1, MOBA_DIM))


def _retention_kernel(lg_ref, q_ref, k_ref, v_ref, gate_ref, g_ref, o_ref, state_ref, *, n_chunks):
    C = RET_CHUNK
    d = RET_DIM
    k_scale = d ** -0.5
    heads = range(RET_HEADS)
    cols = lambda h: slice(h * d, (h + 1) * d)
    nt = (((1,), (1,)), ((), ()))
    tn = (((0,), (0,)), ((), ()))

    @pl.when(pl.program_id(1) == 0)
    def _():
        state_ref[...] = jnp.zeros_like(state_ref)

    row = lax.broadcasted_iota(jnp.int32, (C, C), 0)
    col = lax.broadcasted_iota(jnp.int32, (C, C), 1)
    diff = (row - col).astype(F32)
    pos = lax.broadcasted_iota(jnp.int32, (C, 1), 0).astype(F32)
    inner_decay, q_decay, k_decay, chunk_decay = [], [], [], []
    for h in heads:
        lg = lg_ref[h]
        inner_decay.append(jnp.where(diff >= 0, jnp.exp(lg * jnp.maximum(diff, 0.0)), 0.0) * k_scale)
        q_decay.append(jnp.exp(lg * (pos + 1.0)))
        k_decay.append(jnp.exp(lg * (C - 1.0 - pos)) * k_scale)
        chunk_decay.append(jnp.exp(jnp.full((1, 1), lg * C, F32)))

    for n in range(n_chunks):
        sl = slice(n * C, (n + 1) * C)
        q = [q_ref[sl, cols(h)] for h in heads]
        k = [k_ref[sl, cols(h)] for h in heads]
        v = [v_ref[sl, cols(h)] for h in heads]
        state = [state_ref[h] for h in heads]
        raw = [lax.dot_general(q[h], k[h], nt, preferred_element_type=F32) for h in heads]
        cross = [jnp.dot(q[h], state[h].astype(BF16), preferred_element_type=F32) for h in heads]
        inner = [jnp.dot((raw[h] * inner_decay[h]).astype(BF16), v[h], preferred_element_type=F32) for h in heads]
        for h in heads:
            kd = (k[h].astype(F32) * k_decay[h]).astype(BF16)
            kv = lax.dot_general(kd, v[h], tn, preferred_element_type=F32)
            state_ref[h] = state[h] * chunk_decay[h] + kv
        for h in heads:
            y = inner[h] + cross[h] * q_decay[h]
            ms = jnp.mean(y * y, axis=-1, keepdims=True)
            y = y * lax.rsqrt(ms + NORM_EPS) * g_ref[h:h + 1, :]
            gt = gate_ref[sl, cols(h)]
            o_ref[sl, cols(h)] = (y * (gt * jax.nn.sigmoid(gt))).astype(BF16)


def _retention(proj, gate, ret_g, log_decay, B, S, t):
    T = B * S
    ts = t["ret_ts"]
    width = RET_HEADS * RET_DIM
    assert width == GROUP and S % ts == 0 and ts % RET_CHUNK == 0
    tiles = S // ts
    blk = lambda group: pl.BlockSpec((ts, width), lambda b, s: (b * tiles + s, group))
    return pl.pallas_call(
        functools.partial(_retention_kernel, n_chunks=ts // RET_CHUNK),
        grid=(B, tiles),
        in_specs=[
            pl.BlockSpec(memory_space=pltpu.SMEM),
            blk(0), blk(1), blk(2),
            pl.BlockSpec((ts, width), lambda b, s: (b * tiles + s, 0)),
            pl.BlockSpec((RET_HEADS, RET_DIM), lambda b, s: (0, 0)),
        ],
        out_specs=pl.BlockSpec((ts, width), lambda b, s: (b * tiles + s, 0)),
        out_shape=jax.ShapeDtypeStruct((T, width), BF16),
        scratch_shapes=[pltpu.VMEM((RET_HEADS, RET_DIM, RET_DIM), F32)],
        compiler_params=_params(("arbitrary", "arbitrary"), t["vmem_limit"]),
    )(log_decay, proj, proj, proj, gate, ret_g)


def _moba_kernel(slope_ref, q_ref, k_ref, v_ref, o_ref, kmh_ref, kml_ref, vt_ref, kx_ref, qa_ref,
                 *, n_blocks, heads):
    hg = pl.program_id(1)
    step = pl.program_id(2)
    blocks_per_step = q_ref.shape[0] // MOBA_BLOCK
    L = MOBA_BLOCK
    dh = MOBA_DIM
    lanes = lambda g: slice(g * dh, (g + 1) * dh)
    CONST_ROWS = 8
    MASK_ROW0 = CONST_ROWS
    ONES_ROWS = vt_ref.shape[1] - dh

    @pl.when(step == 0)
    def _():
        lane = lax.broadcasted_iota(jnp.int32, (L, dh), 1)
        key = lax.broadcasted_iota(jnp.int32, (L, dh), 0)
        for n in range(n_blocks):
            kx = jnp.where(lane < 3, key, jnp.where(lane < 6, n * L, jnp.where(lane == MASK_ROW0 + n, 1, 0)))
            kx_ref[n * L:(n + 1) * L, :] = kx.astype(F32).astype(BF16)
        for g in range(heads):
            for n in range(n_blocks):
                rows = slice(n * L, (n + 1) * L)
                kb = k_ref[rows, lanes(g)].astype(F32)
                mean = jnp.sum(kb, axis=0, keepdims=True) * (1.0 / L)
                hi = mean.astype(BF16)
                kmh_ref[g, n:n + 1, :] = hi
                kml_ref[g, n:n + 1, :] = (mean - hi.astype(F32)).astype(BF16)
                vt_ref[g, :dh, rows] = v_ref[rows, lanes(g)].astype(F32).T.astype(BF16)
            vt_ref[g, dh:, :] = jnp.ones((ONES_ROWS, vt_ref.shape[2]), BF16)

    def query_block(i, rows):
        blk_id = lax.broadcasted_iota(jnp.int32, (n_blocks, L), 0)
        blk_f = blk_id.astype(F32)
        past = blk_id < i
        base = pl.multiple_of(i * L, L)
        part_row = lax.broadcasted_iota(jnp.int32, (CONST_ROWS, L), 0)
        causal = (lax.broadcasted_iota(jnp.int32, (L, L), 1) >= lax.broadcasted_iota(jnp.int32, (L, L), 0))

        qts = [q_ref[rows, lanes(g)].astype(F32).T.astype(BF16) for g in range(heads)]
        gates = [jnp.dot(kmh_ref[g], qts[g], preferred_element_type=F32)
                 + jnp.dot(kml_ref[g], qts[g], preferred_element_type=F32) for g in range(heads)]
        for g in range(heads):
            gate = jnp.where(past, gates[g], -jnp.inf)
            chosen = jnp.zeros((n_blocks, L), jnp.bool_)
            for _ in range(MOBA_TOPK):
                best = jnp.max(gate, axis=0, keepdims=True)
                first = jnp.min(jnp.where(gate == best, blk_f, float(n_blocks)), axis=0, keepdims=True)
                hit = blk_f == first
                chosen = jnp.logical_or(chosen, hit)
                gate = jnp.where(hit, -jnp.inf, gate)
            mask_rows = jnp.where(jnp.logical_and(chosen, past), 0.0, NEG_INF)
            c2 = jnp.full((CONST_ROWS, L), slope_ref[hg * heads + g] * LOG2E, F32)
            c_hi = c2.astype(BF16).astype(F32)
            c_mid = (c2 - c_hi).astype(BF16).astype(F32)
            c_lo = c2 - c_hi - c_mid
            sel3 = part_row % 3
            const_rows = jnp.where(part_row < 6, jnp.where(sel3 == 0, c_hi, jnp.where(sel3 == 1, c_mid, c_lo)), 0.0)
            extra = jnp.concatenate(
                [const_rows, mask_rows, jnp.zeros((dh - CONST_ROWS - n_blocks, L), F32)], axis=0).astype(BF16)
            qa_ref[g, :dh, :] = qts[g]
            qa_ref[g, dh:, :] = extra

        def scores(g, off, n_keys, kx):
            k_aug = jnp.concatenate([k_ref[pl.ds(off, n_keys), lanes(g)], kx], axis=1)
            return jnp.dot(k_aug, qa_ref[g], preferred_element_type=F32)

        def pair_scores(jj):
            off = pl.multiple_of(jj * (2 * L), 2 * L)
            kx = kx_ref[pl.ds(off, 2 * L), :]
            return [scores(g, off, 2 * L, kx) for g in range(heads)]

        kx_lane = lax.broadcasted_iota(jnp.int32, (L, dh), 1)
        kx_own = jnp.where(kx_lane < MASK_ROW0, kx_ref[pl.ds(base, L), :].astype(F32), 0.0).astype(BF16)
        own = [scores(g, base, L, kx_own) for g in range(heads)]
        first = pair_scores(0)
        init = []
        for g in range(heads):
            s = jnp.where(causal, own[g], NEG_INF)
            m0 = jnp.max(s, axis=0, keepdims=True)
            p = jnp.exp2(s - m0)
            acc0 = jnp.dot(vt_ref[g, :, pl.ds(base, L)], p.astype(BF16), preferred_element_type=F32)
            init += [m0, acc0]

        def fold_pair(jj, raw, carry):
            off = pl.multiple_of(jj * (2 * L), 2 * L)
            out = []
            for g in range(heads):
                m_prev, acc = carry[2 * g:2 * g + 2]
                s = raw[g]
                m_new = jnp.maximum(m_prev, jnp.max(s, axis=0, keepdims=True))
                alpha = jnp.exp2(m_prev - m_new)
                p = jnp.exp2(s - m_new)
                pv = jnp.dot(vt_ref[g, :, pl.ds(off, 2 * L)], p.astype(BF16), preferred_element_type=F32)
                out += [m_new, alpha * acc + pv]
            return tuple(out)

        def past_pair(jj, carry):
            return fold_pair(jj, pair_scores(jj), carry)

        fin = lax.fori_loop(1, (i + 1) // 2, past_pair, fold_pair(0, first, tuple(init)))
        for g in range(heads):
            acc = fin[2 * g + 1]
            o_ref[rows, lanes(g)] = (acc[:dh] / acc[dh:dh + 1]).T.astype(BF16)

    for r in range(blocks_per_step):
        query_block(step * blocks_per_step + r, slice(r * L, (r + 1) * L))


def _moba(proj, slopes, B, S, t):
    T = B * S
    nb = S // MOBA_BLOCK
    G = t["moba_heads_per_step"]
    R = t["moba_query_blocks_per_step"]
    per = GROUP // (G * MOBA_DIM)
    q_off, k_off, v_off = 3 * per, 4 * per, 5 * per
    return pl.pallas_call(
        functools.partial(_moba_kernel, n_blocks=nb, heads=G),
        grid=(B, MOBA_HEADS // G, nb // R),
        in_specs=[
            pl.BlockSpec(memory_space=pltpu.SMEM),
            pl.BlockSpec((R * MOBA_BLOCK, G * MOBA_DIM), lambda b, h, i: (b * (nb // R) + i, q_off + h)),
            pl.BlockSpec((S, G * MOBA_DIM), lambda b, h, i: (b, k_off + h)),
            pl.BlockSpec((S, G * MOBA_DIM), lambda b, h, i: (b, v_off + h)),
        ],
        out_specs=pl.BlockSpec((R * MOBA_BLOCK, G * MOBA_DIM), lambda b, h, i: (b * (nb // R) + i, h)),
        out_shape=jax.ShapeDtypeStruct((T, MOBA_HEADS * MOBA_DIM), BF16),
        scratch_shapes=[
            pltpu.VMEM((G, nb, MOBA_DIM), BF16),
            pltpu.VMEM((G, nb, MOBA_DIM), BF16),
            pltpu.VMEM((G, MOBA_DIM + V7X_BF16_SUBLANES, S), BF16),
            pltpu.VMEM((S, MOBA_DIM), BF16),
            pltpu.VMEM((G, 2 * MOBA_DIM, MOBA_BLOCK), BF16),
        ],
        compiler_params=_params(("arbitrary", "arbitrary", "arbitrary"), t["vmem_limit"]),
    )(slopes, proj, proj, proj)


def _out_proj_kernel(yr_ref, ym_ref, w_ref, x_ref, o_ref):
    kr = yr_ref.shape[1]
    cw = V7X_MXU_COLS
    for c in range(o_ref.shape[1] // cw):
        cols = slice(c * cw, (c + 1) * cw)
        o_ref[:, cols] = (x_ref[:, cols]
                          + jnp.dot(yr_ref[...], w_ref[:kr, cols], preferred_element_type=F32)
                          + jnp.dot(ym_ref[...], w_ref[kr:, cols], preferred_element_type=F32))


def _out_proj(y_r, y_m, w_out, layer, x2, t):
    T, D = x2.shape
    tm = t["out_tm"]
    kr, km = y_r.shape[1], y_m.shape[1]
    vmem = (kr + km) * D * 2 + 2 * tm * (kr + km) * 2 + 4 * tm * D * 4
    return pl.pallas_call(
        _out_proj_kernel,
        grid=(T // tm,),
        in_specs=[
            pl.BlockSpec((tm, kr), lambda i: (i, 0)),
            pl.BlockSpec((tm, km), lambda i: (i, 0)),
            pl.BlockSpec((None, kr + km, D), lambda i: (layer, 0, 0), pipeline_mode=pl.Buffered(1)),
            pl.BlockSpec((tm, D), lambda i: (i, 0)),
        ],
        out_specs=pl.BlockSpec((tm, D), lambda i: (i, 0)),
        out_shape=jax.ShapeDtypeStruct((T, D), F32),
        compiler_params=_params(("arbitrary",), _vmem_request(vmem)),
    )(y_r, y_m, w_out, x2)


HALO = V7X_BF16_SUBLANES


def _ffn_kernel(x_ref, xh_ref, g_ref, wg_ref, wu_ref, cw_ref, cb_ref, wd_ref, o_ref, h_ref, *, tm, tiles_per_seq):
    i = pl.program_id(0)
    f = pl.program_id(1)

    def norm(x):
        ms = jnp.mean(x * x, axis=-1, keepdims=True)
        return (x * lax.rsqrt(ms + NORM_EPS) * g_ref[...]).astype(BF16)

    @pl.when(f == 0)
    def _():
        x = x_ref[...]
        h_ref[HALO:, :] = norm(x)
        seq_start = (i % tiles_per_seq) == 0
        h_ref[:HALO, :] = jnp.where(seq_start, jnp.zeros((HALO, x.shape[1]), BF16), norm(xh_ref[...]))
        o_ref[...] = x

    a = jnp.dot(h_ref[...], wg_ref[...], preferred_element_type=F32)
    u = jnp.dot(h_ref[HALO:, :], wu_ref[...], preferred_element_type=F32)
    conv = cb_ref[...]
    for tap in range(CONV_WIDTH):
        lo = HALO - (CONV_WIDTH - 1) + tap
        conv = conv + cw_ref[tap:tap + 1, :] * a[lo:lo + tm, :]
    act = (conv * jax.nn.sigmoid(conv) * u).astype(BF16)
    o_ref[...] += jnp.dot(act, wd_ref[...], preferred_element_type=F32)


def _conv_ffn(x2, ln_g, w_gate, w_up, conv_w, conv_b, w_down, layer, S, t):
    T, D = x2.shape
    F = w_gate.shape[2]
    tm, tf = t["ffn_tm"], t["ffn_tf"]
    halo_blocks = tm // HALO
    return pl.pallas_call(
        functools.partial(_ffn_kernel, tm=tm, tiles_per_seq=S // tm),
        grid=(T // tm, F // tf),
        in_specs=[
            pl.BlockSpec((tm, D), lambda i, f: (i, 0)),
            pl.BlockSpec((HALO, D), lambda i, f: (jnp.maximum(i * halo_blocks - 1, 0), 0)),
            pl.BlockSpec((1, D), lambda i, f: (0, 0)),
            pl.BlockSpec((None, D, tf), lambda i, f: (layer, 0, f)),
            pl.BlockSpec((None, D, tf), lambda i, f: (layer, 0, f)),
            pl.BlockSpec((CONV_WIDTH, tf), lambda i, f: (0, f)),
            pl.BlockSpec((1, tf), lambda i, f: (0, f)),
            pl.BlockSpec((None, tf, D), lambda i, f: (layer, f, 0)),
        ],
        out_specs=pl.BlockSpec((tm, D), lambda i, f: (i, 0)),
        out_shape=jax.ShapeDtypeStruct((T, D), F32),
        scratch_shapes=[pltpu.VMEM((HALO + tm, D), BF16)],
        compiler_params=_params(("arbitrary", "arbitrary"), t["vmem_limit"]),
    )(x2, x2, ln_g.reshape(1, D), w_gate, w_up, conv_w, conv_b.reshape(1, F), w_down)


def kernel(x, ln1_g, w_in, ret_norm_g, q_norm_g, k_norm_g, w_out, ln2_g, w_gate, w_up, conv_w, conv_b, w_down):
    B, S, D = x.shape
    depth = w_in.shape[0]
    t = _tiles()
    assert S % RET_CHUNK == 0 and S % MOBA_BLOCK == 0 and S % t["ffn_tm"] == 0
    assert w_in.shape[2] == 7 * GROUP

    log_decay = jnp.log(1.0 - 2.0 ** (-5.0 - jnp.arange(RET_HEADS, dtype=F32)))
    slopes = 2.0 ** (-8.0 * (jnp.arange(MOBA_HEADS, dtype=F32) + 1.0) / MOBA_HEADS)

    w_in, w_out, w_gate, w_up, w_down = (w.astype(BF16) for w in (w_in, w_out, w_gate, w_up, w_down))
    x2 = x.reshape(B * S, D)
    for l in range(depth):
        proj, gate = _in_proj(x2, ln1_g[l], w_in, l, q_norm_g[l], k_norm_g[l], t)
        y_r = _retention(proj, gate, ret_norm_g[l], log_decay, B, S, t)
        y_m = _moba(proj, slopes, B, S, t)
        x2 = _out_proj(y_r, y_m, w_out, l, x2, t)
        x2 = _conv_ffn(x2, ln2_g[l], w_gate, w_up, conv_w[l], conv_b[l], w_down, l, S, t)
    return x2.reshape(B, S, D)
```

```python
import functools

import jax
import jax.numpy as jnp
from jax import lax
from jax.experimental import pallas as pl
from jax.experimental.pallas import tpu as pltpu

RET_HEADS = 4
RET_DIM = 256
RET_CHUNK = 256
MOBA_HEADS = 8
MOBA_DIM = 128
MOBA_BLOCK = 256
MOBA_TOPK = 3
CONV_WIDTH = 3
NORM_EPS = 1e-6
NEG_INF = -1e30
LOG2E = 1.4426950408889634
GROUP = 1024

V7X_VMEM_BYTES = 64 * 1024 * 1024
V7X_BF16_SUBLANES = 16
V7X_MXU_COLS = 256

F32 = jnp.float32
BF16 = jnp.bfloat16


def _tiles():
    return dict(
        in_tm=512,
        ffn_tm=1024, ffn_tf=512,
        moba_heads_per_step=8,
        moba_query_blocks_per_step=2,
        ret_ts=512,
        vmem_limit=int(V7X_VMEM_BYTES * 0.92),
    )


def _params(semantics, vmem_limit):
    return pltpu.CompilerParams(dimension_semantics=semantics, vmem_limit_bytes=vmem_limit)


def _vmem_request(block_bytes):
    return min(block_bytes + V7X_VMEM_BYTES // 8, V7X_VMEM_BYTES - V7X_VMEM_BYTES // 32)


_GATE_GROUP = 3
_MQ_GROUP = 4
_MK_GROUP = 5


def _in_proj_kernel(x_ref, g_ref, w_ref, qg_ref, kg_ref, o_ref, gate_ref, h_ref):
    x = x_ref[...]
    ms = jnp.mean(x * x, axis=-1, keepdims=True)
    h_ref[...] = (x * lax.rsqrt(ms + NORM_EPS) * g_ref[...]).astype(BF16)

    cw = V7X_MXU_COLS
    gains = {_MQ_GROUP: qg_ref[...] * (MOBA_DIM ** -0.5 * LOG2E), _MK_GROUP: kg_ref[...]}
    for group in range(w_ref.shape[1] // GROUP):
        out_group = group - (group > _GATE_GROUP)
        for c in range(GROUP // cw):
            lo = c * cw
            acc = jnp.dot(h_ref[...], w_ref[:, group * GROUP + lo:group * GROUP + lo + cw],
                          preferred_element_type=F32)
            if group == _GATE_GROUP:
                gate_ref[:, lo:lo + cw] = acc
            elif group in gains:
                for hh in range(cw // MOBA_DIM):
                    seg = acc[:, hh * MOBA_DIM:(hh + 1) * MOBA_DIM]
                    ms = jnp.mean(seg * seg, axis=-1, keepdims=True)
                    at = out_group * GROUP + lo + hh * MOBA_DIM
                    o_ref[:, at:at + MOBA_DIM] = (seg * lax.rsqrt(ms + NORM_EPS) * gains[group]).astype(BF16)
            else:
                o_ref[:, out_group * GROUP + lo:out_group * GROUP + lo + cw] = acc.astype(BF16)


def _in_proj(x2, ln_g, w_in, layer, q_g, k_g, t):
    T, D = x2.shape
    n_cols = w_in.shape[2]
    tm = t["in_tm"]
    n_out = n_cols - GROUP
    vmem = D * n_cols * 2 + 2 * tm * D * 4 + 2 * tm * n_out * 2 + 2 * tm * GROUP * 4 + tm * D * 2
    return pl.pallas_call(
        _in_proj_kernel,
        grid=(T // tm,),
        in_specs=[
            pl.BlockSpec((tm, D), lambda i: (i, 0)),
            pl.BlockSpec((1, D), lambda i: (0, 0)),
            pl.BlockSpec((None, D, n_cols), lambda i: (layer, 0, 0), pipeline_mode=pl.Buffered(1)),
            pl.BlockSpec((1, MOBA_DIM), lambda i: (0, 0)),
            pl.BlockSpec((1, MOBA_DIM), lambda i: (0, 0)),
        ],
        out_specs=[
            pl.BlockSpec((tm, n_out), lambda i: (i, 0)),
            pl.BlockSpec((tm, GROUP), lambda i: (i, 0)),
        ],
        out_shape=[
            jax.ShapeDtypeStruct((T, n_out), BF16),
            jax.ShapeDtypeStruct((T, GROUP), F32),
        ],
        scratch_shapes=[pltpu.VMEM((tm, D), BF16)],
        compiler_params=_params(("arbitrary",), _vmem_request(vmem)),
    )(x2, ln_g.reshape(1, D), w_in, q_g.reshape(1, MOBA_DIM), k_g.reshape(1, MOBA_DIM))


def _ret_out_kernel(lg_ref, q_ref, k_ref, v_ref, gate_ref, g_ref, ym_ref, w_ref, x_ref, o_ref, state_ref, yr_ref,
                    *, n_chunks):
    C = RET_CHUNK
    d = RET_DIM
    k_scale = d ** -0.5
    heads = range(RET_HEADS)
    cols = lambda h: slice(h * d, (h + 1) * d)
    nt = (((1,), (1,)), ((), ()))
    tn = (((0,), (0,)), ((), ()))
    kr = yr_ref.shape[1]
    cw = V7X_MXU_COLS
    out_chunks = o_ref.shape[1] // cw
    assert out_chunks % n_chunks == 0
    ocols = lambda c: slice(c * cw, (c + 1) * cw)

    @pl.when(pl.program_id(1) == 0)
    def _():
        state_ref[...] = jnp.zeros_like(state_ref)

    row = lax.broadcasted_iota(jnp.int32, (C, C), 0)
    col = lax.broadcasted_iota(jnp.int32, (C, C), 1)
    diff = (row - col).astype(F32)
    pos = lax.broadcasted_iota(jnp.int32, (C, 1), 0).astype(F32)
    inner_decay, q_decay, k_decay, chunk_decay = [], [], [], []
    for h in heads:
        lg = lg_ref[h]
        inner_decay.append(jnp.where(diff >= 0, jnp.exp(lg * jnp.maximum(diff, 0.0)), 0.0) * k_scale)
        q_decay.append(jnp.exp(lg * (pos + 1.0)))
        k_decay.append(jnp.exp(lg * (C - 1.0 - pos)) * k_scale)
        chunk_decay.append(jnp.exp(jnp.full((1, 1), lg * C, F32)))

    for n in range(n_chunks):
        sl = slice(n * C, (n + 1) * C)
        q = [q_ref[sl, cols(h)] for h in heads]
        k = [k_ref[sl, cols(h)] for h in heads]
        v = [v_ref[sl, cols(h)] for h in heads]
        state = [state_ref[h] for h in heads]
        raw = [lax.dot_general(q[h], k[h], nt, preferred_element_type=F32) for h in heads]
        cross = [jnp.dot(q[h], state[h].astype(BF16), preferred_element_type=F32) for h in heads]
        inner = [jnp.dot((raw[h] * inner_decay[h]).astype(BF16), v[h], preferred_element_type=F32) for h in heads]
        for h in heads:
            kd = (k[h].astype(F32) * k_decay[h]).astype(BF16)
            kv = lax.dot_general(kd, v[h], tn, preferred_element_type=F32)
            state_ref[h] = state[h] * chunk_decay[h] + kv
        for c in range(n * out_chunks // n_chunks, (n + 1) * out_chunks // n_chunks):
            o_ref[:, ocols(c)] = x_ref[:, ocols(c)] + jnp.dot(ym_ref[...], w_ref[kr:, ocols(c)],
                                                              preferred_element_type=F32)
        for h in heads:
            y = inner[h] + cross[h] * q_decay[h]
            ms = jnp.mean(y * y, axis=-1, keepdims=True)
            y = y * lax.rsqrt(ms + NORM_EPS) * g_ref[h:h + 1, :]
            gt = gate_ref[sl, cols(h)]
            yr_ref[sl, cols(h)] = (y * (gt * jax.nn.sigmoid(gt))).astype(BF16)

    for c in range(out_chunks):
        o_ref[:, ocols(c)] += jnp.dot(yr_ref[...], w_ref[:kr, ocols(c)], preferred_element_type=F32)


def _ret_out(proj, gate, ret_g, log_decay, y_m, w_out, layer, x2, B, S, t):
    T, D = x2.shape
    ts = t["ret_ts"]
    width = RET_HEADS * RET_DIM
    km = y_m.shape[1]
    assert width == GROUP and S % ts == 0 and ts % RET_CHUNK == 0
    tiles = S // ts
    tile = lambda last, group=0: pl.BlockSpec((ts, last), lambda b, s: (b * tiles + s, group))
    vmem = ((width + km) * D * 2 + 2 * ts * (3 * width * 2 + width * 4 + km * 2 + 2 * D * 4)
            + ts * width * 2 + RET_HEADS * RET_DIM * RET_DIM * 4)
    return pl.pallas_call(
        functools.partial(_ret_out_kernel, n_chunks=ts // RET_CHUNK),
        grid=(B, tiles),
        in_specs=[
            pl.BlockSpec(memory_space=pltpu.SMEM),
            tile(width, 0), tile(width, 1), tile(width, 2),
            tile(width),
            pl.BlockSpec((RET_HEADS, RET_DIM), lambda b, s: (0, 0)),
            tile(km),
            pl.BlockSpec((None, width + km, D), lambda b, s: (layer, 0, 0), pipeline_mode=pl.Buffered(1)),
            tile(D),
        ],
        out_specs=tile(D),
        out_shape=jax.ShapeDtypeStruct((T, D), F32),
        scratch_shapes=[pltpu.VMEM((RET_HEADS, RET_DIM, RET_DIM), F32), pltpu.VMEM((ts, width), BF16)],
        compiler_params=_params(("arbitrary", "arbitrary"), _vmem_request(vmem)),
    )(log_decay, proj, proj, proj, gate, ret_g, y_m, w_out, x2)


def _moba_kernel(slope_ref, q_ref, k_ref, v_ref, o_ref, kmh_ref, kml_ref, vt_ref, kx_ref, qa_ref,
                 *, n_blocks, heads):
    hg = pl.program_id(1)
    step = pl.program_id(2)
    blocks_per_step = q_ref.shape[0] // MOBA_BLOCK
    L = MOBA_BLOCK
    dh = MOBA_DIM
    lanes = lambda g: slice(g * dh, (g + 1) * dh)
    CONST_ROWS = 8
    MASK_ROW0 = CONST_ROWS
    ONES_ROWS = vt_ref.shape[1] - dh

    @pl.when(step == 0)
    def _():
        lane = lax.broadcasted_iota(jnp.int32, (L, dh), 1)
        key = lax.broadcasted_iota(jnp.int32, (L, dh), 0)
        for n in range(n_blocks):
            kx = jnp.where(lane < 3, key, jnp.where(lane < 6, n * L, jnp.where(lane == MASK_ROW0 + n, 1, 0)))
            kx_ref[n * L:(n + 1) * L, :] = kx.astype(F32).astype(BF16)
        for g in range(heads):
            for n in range(n_blocks):
                rows = slice(n * L, (n + 1) * L)
                kb = k_ref[rows, lanes(g)].astype(F32)
                mean = jnp.sum(kb, axis=0, keepdims=True) * (1.0 / L)
                hi = mean.astype(BF16)
                kmh_ref[g, n:n + 1, :] = hi
                kml_ref[g, n:n + 1, :] = (mean - hi.astype(F32)).astype(BF16)
                vt_ref[g, :dh, rows] = v_ref[rows, lanes(g)].astype(F32).T.astype(BF16)
            vt_ref[g, dh:, :] = jnp.ones((ONES_ROWS, vt_ref.shape[2]), BF16)

    def query_block(i, rows):
        blk_id = lax.broadcasted_iota(jnp.int32, (n_blocks, L), 0)
        blk_f = blk_id.astype(F32)
        past = blk_id < i
        base = pl.multiple_of(i * L, L)
        part_row = lax.broadcasted_iota(jnp.int32, (CONST_ROWS, L), 0)
        causal = (lax.broadcasted_iota(jnp.int32, (L, L), 1) >= lax.broadcasted_iota(jnp.int32, (L, L), 0))

        qts = [q_ref[rows, lanes(g)].astype(F32).T.astype(BF16) for g in range(heads)]
        gates = [jnp.dot(kmh_ref[g], qts[g], preferred_element_type=F32)
                 + jnp.dot(kml_ref[g], qts[g], preferred_element_type=F32) for g in range(heads)]
        for g in range(heads):
            gate = jnp.where(past, gates[g], -jnp.inf)
            chosen = jnp.zeros((n_blocks, L), jnp.bool_)
            for _ in range(MOBA_TOPK):
                best = jnp.max(gate, axis=0, keepdims=True)
                first = jnp.min(jnp.where(gate == best, blk_f, float(n_blocks)), axis=0, keepdims=True)
                hit = blk_f == first
                chosen = jnp.logical_or(chosen, hit)
                gate = jnp.where(hit, -jnp.inf, gate)
            mask_rows = jnp.where(jnp.logical_and(chosen, past), 0.0, NEG_INF)
            c2 = jnp.full((CONST_ROWS, L), slope_ref[hg * heads + g] * LOG2E, F32)
            c_hi = c2.astype(BF16).astype(F32)
            c_mid = (c2 - c_hi).astype(BF16).astype(F32)
            c_lo = c2 - c_hi - c_mid
            sel3 = part_row % 3
            const_rows = jnp.where(part_row < 6, jnp.where(sel3 == 0, c_hi, jnp.where(sel3 == 1, c_mid, c_lo)), 0.0)
            extra = jnp.concatenate(
                [const_rows, mask_rows, jnp.zeros((dh - CONST_ROWS - n_blocks, L), F32)], axis=0).astype(BF16)
            qa_ref[g, :dh, :] = qts[g]
            qa_ref[g, dh:, :] = extra

        def scores(g, off, n_keys, kx):
            k_aug = jnp.concatenate([k_ref[pl.ds(off, n_keys), lanes(g)], kx], axis=1)
            return jnp.dot(k_aug, qa_ref[g], preferred_element_type=F32)

        def pair_scores(jj):
            off = pl.multiple_of(jj * (2 * L), 2 * L)
            kx = kx_ref[pl.ds(off, 2 * L), :]
            return [scores(g, off, 2 * L, kx) for g in range(heads)]

        kx_lane = lax.broadcasted_iota(jnp.int32, (L, dh), 1)
        kx_own = jnp.where(kx_lane < MASK_ROW0, kx_ref[pl.ds(base, L), :].astype(F32), 0.0).astype(BF16)
        own = [scores(g, base, L, kx_own) for g in range(heads)]
        first = pair_scores(0)
        init = []
        for g in range(heads):
            s = jnp.where(causal, own[g], NEG_INF)
            m0 = jnp.max(s, axis=0, keepdims=True)
            p = jnp.exp2(s - m0)
            acc0 = jnp.dot(vt_ref[g, :, pl.ds(base, L)], p.astype(BF16), preferred_element_type=F32)
            init += [m0, acc0]

        def fold_pair(jj, raw, carry):
            off = pl.multiple_of(jj * (2 * L), 2 * L)
            out = []
            for g in range(heads):
                m_prev, acc = carry[2 * g:2 * g + 2]
                s = raw[g]
                m_new = jnp.maximum(m_prev, jnp.max(s, axis=0, keepdims=True))
                alpha = jnp.exp2(m_prev - m_new)
                p = jnp.exp2(s - m_new)
                pv = jnp.dot(vt_ref[g, :, pl.ds(off, 2 * L)], p.astype(BF16), preferred_element_type=F32)
                out += [m_new, alpha * acc + pv]
            return tuple(out)

        def past_pair(jj, carry):
            return fold_pair(jj, pair_scores(jj), carry)

        fin = lax.fori_loop(1, (i + 1) // 2, past_pair, fold_pair(0, first, tuple(init)))
        for g in range(heads):
            acc = fin[2 * g + 1]
            o_ref[rows, lanes(g)] = (acc[:dh] / acc[dh:dh + 1]).T.astype(BF16)

    for r in range(blocks_per_step):
        query_block(step * blocks_per_step + r, slice(r * L, (r + 1) * L))


def _moba(proj, slopes, B, S, t):
    T = B * S
    nb = S // MOBA_BLOCK
    G = t["moba_heads_per_step"]
    R = t["moba_query_blocks_per_step"]
    per = GROUP // (G * MOBA_DIM)
    q_off, k_off, v_off = 3 * per, 4 * per, 5 * per
    return pl.pallas_call(
        functools.partial(_moba_kernel, n_blocks=nb, heads=G),
        grid=(B, MOBA_HEADS // G, nb // R),
        in_specs=[
            pl.BlockSpec(memory_space=pltpu.SMEM),
            pl.BlockSpec((R * MOBA_BLOCK, G * MOBA_DIM), lambda b, h, i: (b * (nb // R) + i, q_off + h)),
            pl.BlockSpec((S, G * MOBA_DIM), lambda b, h, i: (b, k_off + h)),
            pl.BlockSpec((S, G * MOBA_DIM), lambda b, h, i: (b, v_off + h)),
        ],
        out_specs=pl.BlockSpec((R * MOBA_BLOCK, G * MOBA_DIM), lambda b, h, i: (b * (nb // R) + i, h)),
        out_shape=jax.ShapeDtypeStruct((T, MOBA_HEADS * MOBA_DIM), BF16),
        scratch_shapes=[
            pltpu.VMEM((G, nb, MOBA_DIM), BF16),
            pltpu.VMEM((G, nb, MOBA_DIM), BF16),
            pltpu.VMEM((G, MOBA_DIM + V7X_BF16_SUBLANES, S), BF16),
            pltpu.VMEM((S, MOBA_DIM), BF16),
            pltpu.VMEM((G, 2 * MOBA_DIM, MOBA_BLOCK), BF16),
        ],
        compiler_params=_params(("arbitrary", "arbitrary", "arbitrary"), t["vmem_limit"]),
    )(slopes, proj, proj, proj)


HALO = V7X_BF16_SUBLANES


def _ffn_kernel(x_ref, xh_ref, g_ref, wg_ref, wu_ref, cw_ref, cb_ref, wd_ref, o_ref, h_ref, *, tm, tiles_per_seq):
    i = pl.program_id(0)
    f = pl.program_id(1)

    def norm(x):
        ms = jnp.mean(x * x, axis=-1, keepdims=True)
        return (x * lax.rsqrt(ms + NORM_EPS) * g_ref[...]).astype(BF16)

    @pl.when(f == 0)
    def _():
        x = x_ref[...]
        h_ref[HALO:, :] = norm(x)
        seq_start = (i % tiles_per_seq) == 0
        h_ref[:HALO, :] = jnp.where(seq_start, jnp.zeros((HALO, x.shape[1]), BF16), norm(xh_ref[...]))
        o_ref[...] = x

    a = jnp.dot(h_ref[...], wg_ref[...], preferred_element_type=F32)
    u = jnp.dot(h_ref[HALO:, :], wu_ref[...], preferred_element_type=F32)
    conv = cb_ref[...]
    for tap in range(CONV_WIDTH):
        lo = HALO - (CONV_WIDTH - 1) + tap
        conv = conv + cw_ref[tap:tap + 1, :] * a[lo:lo + tm, :]
    act = (conv * jax.nn.sigmoid(conv) * u).astype(BF16)
    o_ref[...] += jnp.dot(act, wd_ref[...], preferred_element_type=F32)


def _conv_ffn(x2, ln_g, w_gate, w_up, conv_w, conv_b, w_down, layer, S, t):
    T, D = x2.shape
    F = w_gate.shape[2]
    tm, tf = t["ffn_tm"], t["ffn_tf"]
    halo_blocks = tm // HALO
    return pl.pallas_call(
        functools.partial(_ffn_kernel, tm=tm, tiles_per_seq=S // tm),
        grid=(T // tm, F // tf),
        in_specs=[
            pl.BlockSpec((tm, D), lambda i, f: (i, 0)),
            pl.BlockSpec((HALO, D), lambda i, f: (jnp.maximum(i * halo_blocks - 1, 0), 0)),
            pl.BlockSpec((1, D), lambda i, f: (0, 0)),
            pl.BlockSpec((None, D, tf), lambda i, f: (layer, 0, f)),
            pl.BlockSpec((None, D, tf), lambda i, f: (layer, 0, f)),
            pl.BlockSpec((CONV_WIDTH, tf), lambda i, f: (0, f)),
            pl.BlockSpec((1, tf), lambda i, f: (0, f)),
            pl.BlockSpec((None, tf, D), lambda i, f: (layer, f, 0)),
        ],
        out_specs=pl.BlockSpec((tm, D), lambda i, f: (i, 0)),
        out_shape=jax.ShapeDtypeStruct((T, D), F32),
        scratch_shapes=[pltpu.VMEM((HALO + tm, D), BF16)],
        compiler_params=_params(("arbitrary", "arbitrary"), t["vmem_limit"]),
    )(x2, x2, ln_g.reshape(1, D), w_gate, w_up, conv_w, conv_b.reshape(1, F), w_down)


def kernel(x, ln1_g, w_in, ret_norm_g, q_norm_g, k_norm_g, w_out, ln2_g, w_gate, w_up, conv_w, conv_b, w_down):
    B, S, D = x.shape
    depth = w_in.shape[0]
    t = _tiles()
    assert S % RET_CHUNK == 0 and S % MOBA_BLOCK == 0 and S % t["ffn_tm"] == 0
    assert w_in.shape[2] == 7 * GROUP

    log_decay = jnp.log(1.0 - 2.0 ** (-5.0 - jnp.arange(RET_HEADS, dtype=F32)))
    slopes = 2.0 ** (-8.0 * (jnp.arange(MOBA_HEADS, dtype=F32) + 1.0) / MOBA_HEADS)

    w_in, w_out, w_gate, w_up, w_down = (w.astype(BF16) for w in (w_in, w_out, w_gate, w_up, w_down))
    x2 = x.reshape(B * S, D)
    for l in range(depth):
        proj, gate = _in_proj(x2, ln1_g[l], w_in, l, q_norm_g[l], k_norm_g[l], t)
        y_m = _moba(proj, slopes, B, S, t)
        x2 = _ret_out(proj, gate, ret_norm_g[l], log_decay, y_m, w_out, l, x2, B, S, t)
        x2 = _conv_ffn(x2, ln2_g[l], w_gate, w_up, conv_w[l], conv_b[l], w_down, l, S, t)
    return x2.reshape(B, S, D)
```

```python
import functools

import jax
import jax.numpy as jnp
from jax import lax
from jax.experimental import pallas as pl
from jax.experimental.pallas import tpu as pltpu

RET_HEADS = 4
RET_DIM = 256
RET_CHUNK = 256
MOBA_HEADS = 8
MOBA_DIM = 128
MOBA_BLOCK = 256
MOBA_TOPK = 3
CONV_WIDTH = 3
NORM_EPS = 1e-6
NEG_INF = -1e30
LOG2E = 1.4426950408889634
GROUP = 1024

V7X_VMEM_BYTES = 64 * 1024 * 1024
V7X_BF16_SUBLANES = 16
V7X_MXU_COLS = 256

F32 = jnp.float32
BF16 = jnp.bfloat16


def _tiles():
    return dict(
        in_tm=512,
        ffn_tm=1024, ffn_tf=512,
        moba_heads_per_step=8,
        moba_query_blocks_per_step=2,
        ret_ts=512,
        vmem_limit=int(V7X_VMEM_BYTES * 0.92),
    )


def _params(semantics, vmem_limit):
    return pltpu.CompilerParams(dimension_semantics=semantics, vmem_limit_bytes=vmem_limit)


def _vmem_request(block_bytes):
    return min(block_bytes + V7X_VMEM_BYTES // 8, V7X_VMEM_BYTES - V7X_VMEM_BYTES // 32)


_GATE_GROUP = 3
_MQ_GROUP = 4
_MK_GROUP = 5


def _in_proj_kernel(x_ref, g_ref, w_ref, qg_ref, kg_ref, o_ref, gate_ref, h_ref):
    x = x_ref[...]
    ms = jnp.mean(x * x, axis=-1, keepdims=True)
    h_ref[...] = (x * lax.rsqrt(ms + NORM_EPS) * g_ref[...]).astype(BF16)

    cw = V7X_MXU_COLS
    gains = {_MQ_GROUP: qg_ref[...] * (MOBA_DIM ** -0.5 * LOG2E), _MK_GROUP: kg_ref[...]}
    for group in range(w_ref.shape[1] // GROUP):
        out_group = group - (group > _GATE_GROUP)
        for c in range(GROUP // cw):
            lo = c * cw
            acc = jnp.dot(h_ref[...], w_ref[:, group * GROUP + lo:group * GROUP + lo + cw],
                          preferred_element_type=F32)
            if group == _GATE_GROUP:
                gate_ref[:, lo:lo + cw] = acc
            elif group in gains:
                for hh in range(cw // MOBA_DIM):
                    seg = acc[:, hh * MOBA_DIM:(hh + 1) * MOBA_DIM]
                    ms = jnp.mean(seg * seg, axis=-1, keepdims=True)
                    at = out_group * GROUP + lo + hh * MOBA_DIM
                    o_ref[:, at:at + MOBA_DIM] = (seg * lax.rsqrt(ms + NORM_EPS) * gains[group]).astype(BF16)
            else:
                o_ref[:, out_group * GROUP + lo:out_group * GROUP + lo + cw] = acc.astype(BF16)


def _in_proj(x2, ln_g, w_in, layer, q_g, k_g, t):
    T, D = x2.shape
    n_cols = w_in.shape[2]
    tm = t["in_tm"]
    n_out = n_cols - GROUP
    vmem = D * n_cols * 2 + 2 * tm * D * 4 + 2 * tm * n_out * 2 + 2 * tm * GROUP * 4 + tm * D * 2
    return pl.pallas_call(
        _in_proj_kernel,
        grid=(T // tm,),
        in_specs=[
            pl.BlockSpec((tm, D), lambda i: (i, 0)),
            pl.BlockSpec((1, D), lambda i: (0, 0)),
            pl.BlockSpec((None, D, n_cols), lambda i: (layer, 0, 0), pipeline_mode=pl.Buffered(1)),
            pl.BlockSpec((1, MOBA_DIM), lambda i: (0, 0)),
            pl.BlockSpec((1, MOBA_DIM), lambda i: (0, 0)),
        ],
        out_specs=[
            pl.BlockSpec((tm, n_out), lambda i: (i, 0)),
            pl.BlockSpec((tm, GROUP), lambda i: (i, 0)),
        ],
        out_shape=[
            jax.ShapeDtypeStruct((T, n_out), BF16),
            jax.ShapeDtypeStruct((T, GROUP), F32),
        ],
        scratch_shapes=[pltpu.VMEM((tm, D), BF16)],
        compiler_params=_params(("arbitrary",), _vmem_request(vmem)),
    )(x2, ln_g.reshape(1, D), w_in, q_g.reshape(1, MOBA_DIM), k_g.reshape(1, MOBA_DIM))


def _ret_out_kernel(lg_ref, q_ref, k_ref, v_ref, gate_ref, g_ref, ym_ref, w_ref, x_ref, wg_ref, wu_ref, wd_ref,
                    o_ref, wg_o, wu_o, wd_o, state_ref, yr_ref, *, n_chunks):
    C = RET_CHUNK
    d = RET_DIM
    k_scale = d ** -0.5
    heads = range(RET_HEADS)
    cols = lambda h: slice(h * d, (h + 1) * d)
    nt = (((1,), (1,)), ((), ()))
    tn = (((0,), (0,)), ((), ()))
    kr = yr_ref.shape[1]
    cw = V7X_MXU_COLS
    out_chunks = o_ref.shape[1] // cw
    assert out_chunks % n_chunks == 0
    ocols = lambda c: slice(c * cw, (c + 1) * cw)

    @pl.when(pl.program_id(1) == 0)
    def _():
        state_ref[...] = jnp.zeros_like(state_ref)

    row = lax.broadcasted_iota(jnp.int32, (C, C), 0)
    col = lax.broadcasted_iota(jnp.int32, (C, C), 1)
    diff = (row - col).astype(F32)
    pos = lax.broadcasted_iota(jnp.int32, (C, 1), 0).astype(F32)
    inner_decay, q_decay, k_decay, chunk_decay = [], [], [], []
    for h in heads:
        lg = lg_ref[h]
        inner_decay.append(jnp.where(diff >= 0, jnp.exp(lg * jnp.maximum(diff, 0.0)), 0.0) * k_scale)
        q_decay.append(jnp.exp(lg * (pos + 1.0)))
        k_decay.append(jnp.exp(lg * (C - 1.0 - pos)) * k_scale)
        chunk_decay.append(jnp.exp(jnp.full((1, 1), lg * C, F32)))

    for n in range(n_chunks):
        sl = slice(n * C, (n + 1) * C)
        q = [q_ref[sl, cols(h)] for h in heads]
        k = [k_ref[sl, cols(h)] for h in heads]
        v = [v_ref[sl, cols(h)] for h in heads]
        state = [state_ref[h] for h in heads]
        raw = [lax.dot_general(q[h], k[h], nt, preferred_element_type=F32) for h in heads]
        cross = [jnp.dot(q[h], state[h].astype(BF16), preferred_element_type=F32) for h in heads]
        inner = [jnp.dot((raw[h] * inner_decay[h]).astype(BF16), v[h], preferred_element_type=F32) for h in heads]
        for h in heads:
            kd = (k[h].astype(F32) * k_decay[h]).astype(BF16)
            kv = lax.dot_general(kd, v[h], tn, preferred_element_type=F32)
            state_ref[h] = state[h] * chunk_decay[h] + kv
        for c in range(n * out_chunks // n_chunks, (n + 1) * out_chunks // n_chunks):
            o_ref[:, ocols(c)] = x_ref[:, ocols(c)] + jnp.dot(ym_ref[...], w_ref[kr:, ocols(c)],
                                                              preferred_element_type=F32)
        for h in heads:
            y = inner[h] + cross[h] * q_decay[h]
            ms = jnp.mean(y * y, axis=-1, keepdims=True)
            y = y * lax.rsqrt(ms + NORM_EPS) * g_ref[h:h + 1, :]
            gt = gate_ref[sl, cols(h)]
            yr_ref[sl, cols(h)] = (y * (gt * jax.nn.sigmoid(gt))).astype(BF16)

    for c in range(out_chunks):
        o_ref[:, ocols(c)] += jnp.dot(yr_ref[...], w_ref[:kr, ocols(c)], preferred_element_type=F32)

    wg_o[...] = wg_ref[...].astype(BF16)
    wu_o[...] = wu_ref[...].astype(BF16)
    wd_o[...] = wd_ref[...].astype(BF16)


def _ret_out(proj, gate, ret_g, log_decay, y_m, w_out, w_gate, w_up, w_down, layer, x2, B, S, t):
    T, D = x2.shape
    ts = t["ret_ts"]
    width = RET_HEADS * RET_DIM
    km = y_m.shape[1]
    assert width == GROUP and S % ts == 0 and ts % RET_CHUNK == 0
    tiles = S // ts
    tile = lambda last, group=0: pl.BlockSpec((ts, last), lambda b, s: (b * tiles + s, group))
    steps = B * tiles
    F = w_gate.shape[2]
    assert D % (steps * V7X_BF16_SUBLANES) == 0 and F % (steps * V7X_BF16_SUBLANES) == 0
    up_rows, down_rows = D // steps, F // steps
    w_slice = lambda rows, width_: pl.BlockSpec((None, rows, width_), lambda b, s: (layer, b * tiles + s, 0))
    w_cast = lambda rows, width_: pl.BlockSpec((rows, width_), lambda b, s: (b * tiles + s, 0))
    vmem = ((width + km) * D * 2 + 2 * ts * (3 * width * 2 + width * 4 + km * 2 + 2 * D * 4)
            + ts * width * 2 + RET_HEADS * RET_DIM * RET_DIM * 4
            + 2 * (2 * up_rows * F + down_rows * D) * (4 + 2))
    return pl.pallas_call(
        functools.partial(_ret_out_kernel, n_chunks=ts // RET_CHUNK),
        grid=(B, tiles),
        in_specs=[
            pl.BlockSpec(memory_space=pltpu.SMEM),
            tile(width, 0), tile(width, 1), tile(width, 2),
            tile(width),
            pl.BlockSpec((RET_HEADS, RET_DIM), lambda b, s: (0, 0)),
            tile(km),
            pl.BlockSpec((None, width + km, D), lambda b, s: (layer, 0, 0), pipeline_mode=pl.Buffered(1)),
            tile(D),
            w_slice(up_rows, F), w_slice(up_rows, F), w_slice(down_rows, D),
        ],
        out_specs=[tile(D), w_cast(up_rows, F), w_cast(up_rows, F), w_cast(down_rows, D)],
        out_shape=[jax.ShapeDtypeStruct((T, D), F32), jax.ShapeDtypeStruct((D, F), BF16),
                   jax.ShapeDtypeStruct((D, F), BF16), jax.ShapeDtypeStruct((F, D), BF16)],
        scratch_shapes=[pltpu.VMEM((RET_HEADS, RET_DIM, RET_DIM), F32), pltpu.VMEM((ts, width), BF16)],
        compiler_params=_params(("arbitrary", "arbitrary"), _vmem_request(vmem)),
    )(log_decay, proj, proj, proj, gate, ret_g, y_m, w_out, x2, w_gate, w_up, w_down)


def _moba_kernel(slope_ref, q_ref, k_ref, v_ref, o_ref, kmh_ref, kml_ref, vt_ref, kx_ref, qa_ref,
                 *, n_blocks, heads):
    hg = pl.program_id(1)
    step = pl.program_id(2)
    blocks_per_step = q_ref.shape[0] // MOBA_BLOCK
    L = MOBA_BLOCK
    dh = MOBA_DIM
    lanes = lambda g: slice(g * dh, (g + 1) * dh)
    CONST_ROWS = 8
    MASK_ROW0 = CONST_ROWS
    ONES_ROWS = vt_ref.shape[1] - dh

    @pl.when(step == 0)
    def _():
        lane = lax.broadcasted_iota(jnp.int32, (L, dh), 1)
        key = lax.broadcasted_iota(jnp.int32, (L, dh), 0)
        for n in range(n_blocks):
            kx = jnp.where(lane < 3, key, jnp.where(lane < 6, n * L, jnp.where(lane == MASK_ROW0 + n, 1, 0)))
            kx_ref[n * L:(n + 1) * L, :] = kx.astype(F32).astype(BF16)
        for g in range(heads):
            for n in range(n_blocks):
                rows = slice(n * L, (n + 1) * L)
                kb = k_ref[rows, lanes(g)].astype(F32)
                mean = jnp.sum(kb, axis=0, keepdims=True) * (1.0 / L)
                hi = mean.astype(BF16)
                kmh_ref[g, n:n + 1, :] = hi
                kml_ref[g, n:n + 1, :] = (mean - hi.astype(F32)).astype(BF16)
                vt_ref[g, :dh, rows] = v_ref[rows, lanes(g)].astype(F32).T.astype(BF16)
            vt_ref[g, dh:, :] = jnp.ones((ONES_ROWS, vt_ref.shape[2]), BF16)

    def query_block(i, rows):
        blk_id = lax.broadcasted_iota(jnp.int32, (n_blocks, L), 0)
        blk_f = blk_id.astype(F32)
        past = blk_id < i
        base = pl.multiple_of(i * L, L)
        part_row = lax.broadcasted_iota(jnp.int32, (CONST_ROWS, L), 0)
        causal = (lax.broadcasted_iota(jnp.int32, (L, L), 1) >= lax.broadcasted_iota(jnp.int32, (L, L), 0))

        qts = [q_ref[rows, lanes(g)].astype(F32).T.astype(BF16) for g in range(heads)]
        gates = [jnp.dot(kmh_ref[g], qts[g], preferred_element_type=F32)
                 + jnp.dot(kml_ref[g], qts[g], preferred_element_type=F32) for g in range(heads)]
        for g in range(heads):
            gate = jnp.where(past, gates[g], -jnp.inf)
            chosen = jnp.zeros((n_blocks, L), jnp.bool_)
            for _ in range(MOBA_TOPK):
                best = jnp.max(gate, axis=0, keepdims=True)
                first = jnp.min(jnp.where(gate == best, blk_f, float(n_blocks)), axis=0, keepdims=True)
                hit = blk_f == first
                chosen = jnp.logical_or(chosen, hit)
                gate = jnp.where(hit, -jnp.inf, gate)
            mask_rows = jnp.where(jnp.logical_and(chosen, past), 0.0, NEG_INF)
            c2 = jnp.full((CONST_ROWS, L), slope_ref[hg * heads + g] * LOG2E, F32)
            c_hi = c2.astype(BF16).astype(F32)
            c_mid = (c2 - c_hi).astype(BF16).astype(F32)
            c_lo = c2 - c_hi - c_mid
            sel3 = part_row % 3
            const_rows = jnp.where(part_row < 6, jnp.where(sel3 == 0, c_hi, jnp.where(sel3 == 1, c_mid, c_lo)), 0.0)
            extra = jnp.concatenate(
                [const_rows, mask_rows, jnp.zeros((dh - CONST_ROWS - n_blocks, L), F32)], axis=0).astype(BF16)
            qa_ref[g, :dh, :] = qts[g]
            qa_ref[g, dh:, :] = extra

        def scores(g, off, n_keys, kx):
            k_aug = jnp.concatenate([k_ref[pl.ds(off, n_keys), lanes(g)], kx], axis=1)
            return jnp.dot(k_aug, qa_ref[g], preferred_element_type=F32)

        def pair_scores(jj):
            off = pl.multiple_of(jj * (2 * L), 2 * L)
            kx = kx_ref[pl.ds(off, 2 * L), :]
            return [scores(g, off, 2 * L, kx) for g in range(heads)]

        kx_lane = lax.broadcasted_iota(jnp.int32, (L, dh), 1)
        kx_own = jnp.where(kx_lane < MASK_ROW0, kx_ref[pl.ds(base, L), :].astype(F32), 0.0).astype(BF16)
        own = [scores(g, base, L, kx_own) for g in range(heads)]
        first = pair_scores(0)
        init = []
        for g in range(heads):
            s = jnp.where(causal, own[g], NEG_INF)
            m0 = jnp.max(s, axis=0, keepdims=True)
            p = jnp.exp2(s - m0)
            acc0 = jnp.dot(vt_ref[g, :, pl.ds(base, L)], p.astype(BF16), preferred_element_type=F32)
            init += [m0, acc0]

        def fold_pair(jj, raw, carry):
            off = pl.multiple_of(jj * (2 * L), 2 * L)
            out = []
            for g in range(heads):
                m_prev, acc = carry[2 * g:2 * g + 2]
                s = raw[g]
                m_new = jnp.maximum(m_prev, jnp.max(s, axis=0, keepdims=True))
                alpha = jnp.exp2(m_prev - m_new)
                p = jnp.exp2(s - m_new)
                pv = jnp.dot(vt_ref[g, :, pl.ds(off, 2 * L)], p.astype(BF16), preferred_element_type=F32)
                out += [m_new, alpha * acc + pv]
            return tuple(out)

        def past_pair(jj, carry):
            return fold_pair(jj, pair_scores(jj), carry)

        fin = lax.fori_loop(1, (i + 1) // 2, past_pair, fold_pair(0, first, tuple(init)))
        for g in range(heads):
            acc = fin[2 * g + 1]
            o_ref[rows, lanes(g)] = (acc[:dh] / acc[dh:dh + 1]).T.astype(BF16)

    for r in range(blocks_per_step):
        query_block(step * blocks_per_step + r, slice(r * L, (r + 1) * L))


def _moba(proj, slopes, B, S, t):
    T = B * S
    nb = S // MOBA_BLOCK
    G = t["moba_heads_per_step"]
    R = t["moba_query_blocks_per_step"]
    per = GROUP // (G * MOBA_DIM)
    q_off, k_off, v_off = 3 * per, 4 * per, 5 * per
    return pl.pallas_call(
        functools.partial(_moba_kernel, n_blocks=nb, heads=G),
        grid=(B, MOBA_HEADS // G, nb // R),
        in_specs=[
            pl.BlockSpec(memory_space=pltpu.SMEM),
            pl.BlockSpec((R * MOBA_BLOCK, G * MOBA_DIM), lambda b, h, i: (b * (nb // R) + i, q_off + h)),
            pl.BlockSpec((S, G * MOBA_DIM), lambda b, h, i: (b, k_off + h)),
            pl.BlockSpec((S, G * MOBA_DIM), lambda b, h, i: (b, v_off + h)),
        ],
        out_specs=pl.BlockSpec((R * MOBA_BLOCK, G * MOBA_DIM), lambda b, h, i: (b * (nb // R) + i, h)),
        out_shape=jax.ShapeDtypeStruct((T, MOBA_HEADS * MOBA_DIM), BF16),
        scratch_shapes=[
            pltpu.VMEM((G, nb, MOBA_DIM), BF16),
            pltpu.VMEM((G, nb, MOBA_DIM), BF16),
            pltpu.VMEM((G, MOBA_DIM + V7X_BF16_SUBLANES, S), BF16),
            pltpu.VMEM((S, MOBA_DIM), BF16),
            pltpu.VMEM((G, 2 * MOBA_DIM, MOBA_BLOCK), BF16),
        ],
        compiler_params=_params(("arbitrary", "arbitrary", "arbitrary"), t["vmem_limit"]),
    )(slopes, proj, proj, proj)


HALO = V7X_BF16_SUBLANES


def _ffn_kernel(x_ref, xh_ref, g_ref, wg_ref, wu_ref, cw_ref, cb_ref, wd_ref, o_ref, h_ref, *, tm, tiles_per_seq):
    i = pl.program_id(0)
    f = pl.program_id(1)

    def norm(x):
        ms = jnp.mean(x * x, axis=-1, keepdims=True)
        return (x * lax.rsqrt(ms + NORM_EPS) * g_ref[...]).astype(BF16)

    @pl.when(f == 0)
    def _():
        x = x_ref[...]
        h_ref[HALO:, :] = norm(x)
        seq_start = (i % tiles_per_seq) == 0
        h_ref[:HALO, :] = jnp.where(seq_start, jnp.zeros((HALO, x.shape[1]), BF16), norm(xh_ref[...]))
        o_ref[...] = x

    a = jnp.dot(h_ref[...], wg_ref[...], preferred_element_type=F32)
    u = jnp.dot(h_ref[HALO:, :], wu_ref[...], preferred_element_type=F32)
    conv = cb_ref[...]
    for tap in range(CONV_WIDTH):
        lo = HALO - (CONV_WIDTH - 1) + tap
        conv = conv + cw_ref[tap:tap + 1, :] * a[lo:lo + tm, :]
    act = (conv * jax.nn.sigmoid(conv) * u).astype(BF16)
    o_ref[...] += jnp.dot(act, wd_ref[...], preferred_element_type=F32)


def _conv_ffn(x2, ln_g, w_gate, w_up, conv_w, conv_b, w_down, S, t):
    T, D = x2.shape
    F = w_gate.shape[1]
    tm, tf = t["ffn_tm"], t["ffn_tf"]
    halo_blocks = tm // HALO
    return pl.pallas_call(
        functools.partial(_ffn_kernel, tm=tm, tiles_per_seq=S // tm),
        grid=(T // tm, F // tf),
        in_specs=[
            pl.BlockSpec((tm, D), lambda i, f: (i, 0)),
            pl.BlockSpec((HALO, D), lambda i, f: (jnp.maximum(i * halo_blocks - 1, 0), 0)),
            pl.BlockSpec((1, D), lambda i, f: (0, 0)),
            pl.BlockSpec((D, tf), lambda i, f: (0, f)),
            pl.BlockSpec((D, tf), lambda i, f: (0, f)),
            pl.BlockSpec((CONV_WIDTH, tf), lambda i, f: (0, f)),
            pl.BlockSpec((1, tf), lambda i, f: (0, f)),
            pl.BlockSpec((tf, D), lambda i, f: (f, 0)),
        ],
        out_specs=pl.BlockSpec((tm, D), lambda i, f: (i, 0)),
        out_shape=jax.ShapeDtypeStruct((T, D), F32),
        scratch_shapes=[pltpu.VMEM((HALO + tm, D), BF16)],
        compiler_params=_params(("arbitrary", "arbitrary"), t["vmem_limit"]),
    )(x2, x2, ln_g.reshape(1, D), w_gate, w_up, conv_w, conv_b.reshape(1, F), w_down)


def kernel(x, ln1_g, w_in, ret_norm_g, q_norm_g, k_norm_g, w_out, ln2_g, w_gate, w_up, conv_w, conv_b, w_down):
    B, S, D = x.shape
    depth = w_in.shape[0]
    t = _tiles()
    assert S % RET_CHUNK == 0 and S % MOBA_BLOCK == 0 and S % t["ffn_tm"] == 0
    assert w_in.shape[2] == 7 * GROUP

    log_decay = jnp.log(1.0 - 2.0 ** (-5.0 - jnp.arange(RET_HEADS, dtype=F32)))
    slopes = 2.0 ** (-8.0 * (jnp.arange(MOBA_HEADS, dtype=F32) + 1.0) / MOBA_HEADS)

    w_in, w_out = w_in.astype(BF16), w_out.astype(BF16)
    x2 = x.reshape(B * S, D)
    for l in range(depth):
        proj, gate = _in_proj(x2, ln1_g[l], w_in, l, q_norm_g[l], k_norm_g[l], t)
        y_m = _moba(proj, slopes, B, S, t)
        x2, wg, wu, wd = _ret_out(proj, gate, ret_norm_g[l], log_decay, y_m, w_out, w_gate, w_up, w_down, l, x2, B, S, t)
        x2 = _conv_ffn(x2, ln2_g[l], wg, wu, conv_w[l], conv_b[l], wd, S, t)
    return x2.reshape(B, S, D)
```

```python
import functools

import jax
import jax.numpy as jnp
from jax import lax
from jax.experimental import pallas as pl
from jax.experimental.pallas import tpu as pltpu

RET_HEADS = 4
RET_DIM = 256
RET_CHUNK = 256
MOBA_HEADS = 8
MOBA_DIM = 128
MOBA_BLOCK = 256
MOBA_TOPK = 3
CONV_WIDTH = 3
NORM_EPS = 1e-6
NEG_INF = -1e30
LOG2E = 1.4426950408889634
GROUP = 1024

V7X_VMEM_BYTES = 64 * 1024 * 1024
V7X_BF16_SUBLANES = 16
V7X_MXU_COLS = 256

F32 = jnp.float32
BF16 = jnp.bfloat16


def _tiles():
    return dict(
        in_tm=512,
        ffn_tm=1024, ffn_tf=512,
        moba_heads_per_step=8,
        moba_query_blocks_per_step=2,
        ret_ts=512,
        vmem_limit=int(V7X_VMEM_BYTES * 0.92),
    )


def _params(semantics, vmem_limit):
    return pltpu.CompilerParams(dimension_semantics=semantics, vmem_limit_bytes=vmem_limit)


def _vmem_request(block_bytes):
    return min(block_bytes + V7X_VMEM_BYTES // 8, V7X_VMEM_BYTES - V7X_VMEM_BYTES // 32)


_GATE_GROUP = 3
_MQ_GROUP = 4
_MK_GROUP = 5


def _in_proj_kernel(x_ref, g_ref, w_ref, qg_ref, kg_ref, o_ref, gate_ref, h_ref):
    x = x_ref[...]
    ms = jnp.mean(x * x, axis=-1, keepdims=True)
    h_ref[...] = (x * lax.rsqrt(ms + NORM_EPS) * g_ref[...]).astype(BF16)

    cw = V7X_MXU_COLS
    gains = {_MQ_GROUP: qg_ref[...] * (MOBA_DIM ** -0.5 * LOG2E), _MK_GROUP: kg_ref[...]}
    for group in range(w_ref.shape[1] // GROUP):
        out_group = group - (group > _GATE_GROUP)
        for c in range(GROUP // cw):
            lo = c * cw
            acc = jnp.dot(h_ref[...], w_ref[:, group * GROUP + lo:group * GROUP + lo + cw],
                          preferred_element_type=F32)
            if group == _GATE_GROUP:
                gate_ref[:, lo:lo + cw] = acc
            elif group in gains:
                for hh in range(cw // MOBA_DIM):
                    seg = acc[:, hh * MOBA_DIM:(hh + 1) * MOBA_DIM]
                    ms = jnp.mean(seg * seg, axis=-1, keepdims=True)
                    at = out_group * GROUP + lo + hh * MOBA_DIM
                    o_ref[:, at:at + MOBA_DIM] = (seg * lax.rsqrt(ms + NORM_EPS) * gains[group]).astype(BF16)
            else:
                o_ref[:, out_group * GROUP + lo:out_group * GROUP + lo + cw] = acc.astype(BF16)


def _in_proj(x2, ln_g, w_in, layer, q_g, k_g, t):
    T, D = x2.shape
    n_cols = w_in.shape[2]
    tm = t["in_tm"]
    n_out = n_cols - GROUP
    vmem = D * n_cols * 2 + 2 * tm * D * 4 + 2 * tm * n_out * 2 + 2 * tm * GROUP * 4 + tm * D * 2
    return pl.pallas_call(
        _in_proj_kernel,
        grid=(T // tm,),
        in_specs=[
            pl.BlockSpec((tm, D), lambda i: (i, 0)),
            pl.BlockSpec((1, D), lambda i: (0, 0)),
            pl.BlockSpec((None, D, n_cols), lambda i: (layer, 0, 0), pipeline_mode=pl.Buffered(1)),
            pl.BlockSpec((1, MOBA_DIM), lambda i: (0, 0)),
            pl.BlockSpec((1, MOBA_DIM), lambda i: (0, 0)),
        ],
        out_specs=[
            pl.BlockSpec((tm, n_out), lambda i: (i, 0)),
            pl.BlockSpec((tm, GROUP), lambda i: (i, 0)),
        ],
        out_shape=[
            jax.ShapeDtypeStruct((T, n_out), BF16),
            jax.ShapeDtypeStruct((T, GROUP), F32),
        ],
        scratch_shapes=[pltpu.VMEM((tm, D), BF16)],
        compiler_params=_params(("arbitrary",), _vmem_request(vmem)),
    )(x2, ln_g.reshape(1, D), w_in, q_g.reshape(1, MOBA_DIM), k_g.reshape(1, MOBA_DIM))


def _ret_out_kernel(lg_ref, q_ref, k_ref, v_ref, gate_ref, g_ref, ym_ref, w_ref, x_ref, wg_ref, wu_ref, wd_ref,
                    o_ref, wg_o, wu_o, wd_o, state_ref, yr_ref, *, n_chunks):
    C = RET_CHUNK
    d = RET_DIM
    k_scale = d ** -0.5
    heads = range(RET_HEADS)
    cols = lambda h: slice(h * d, (h + 1) * d)
    nt = (((1,), (1,)), ((), ()))
    tn = (((0,), (0,)), ((), ()))
    kr = yr_ref.shape[1]
    cw = V7X_MXU_COLS
    out_chunks = o_ref.shape[1] // cw
    assert out_chunks % n_chunks == 0
    ocols = lambda c: slice(c * cw, (c + 1) * cw)

    @pl.when(pl.program_id(1) == 0)
    def _():
        state_ref[...] = jnp.zeros_like(state_ref)

    row = lax.broadcasted_iota(jnp.int32, (C, C), 0)
    col = lax.broadcasted_iota(jnp.int32, (C, C), 1)
    diff = (row - col).astype(F32)
    pos = lax.broadcasted_iota(jnp.int32, (C, 1), 0).astype(F32)
    inner_decay, q_decay, k_decay, chunk_decay = [], [], [], []
    for h in heads:
        lg = lg_ref[h]
        inner_decay.append(jnp.where(diff >= 0, jnp.exp(lg * jnp.maximum(diff, 0.0)), 0.0) * k_scale)
        q_decay.append(jnp.exp(lg * (pos + 1.0)))
        k_decay.append(jnp.exp(lg * (C - 1.0 - pos)) * k_scale)
        chunk_decay.append(jnp.exp(jnp.full((1, 1), lg * C, F32)))

    for n in range(n_chunks):
        sl = slice(n * C, (n + 1) * C)
        q = [q_ref[sl, cols(h)] for h in heads]
        k = [k_ref[sl, cols(h)] for h in heads]
        v = [v_ref[sl, cols(h)] for h in heads]
        state = [state_ref[h] for h in heads]
        raw = [lax.dot_general(q[h], k[h], nt, preferred_element_type=F32) for h in heads]
        cross = [jnp.dot(q[h], state[h].astype(BF16), preferred_element_type=F32) for h in heads]
        inner = [jnp.dot((raw[h] * inner_decay[h]).astype(BF16), v[h], preferred_element_type=F32) for h in heads]
        for h in heads:
            kd = (k[h].astype(F32) * k_decay[h]).astype(BF16)
            kv = lax.dot_general(kd, v[h], tn, preferred_element_type=F32)
            state_ref[h] = state[h] * chunk_decay[h] + kv
        for c in range(n * out_chunks // n_chunks, (n + 1) * out_chunks // n_chunks):
            o_ref[:, ocols(c)] = x_ref[:, ocols(c)] + jnp.dot(ym_ref[...], w_ref[kr:, ocols(c)],
                                                              preferred_element_type=F32)
        for h in heads:
            y = inner[h] + cross[h] * q_decay[h]
            ms = jnp.mean(y * y, axis=-1, keepdims=True)
            y = y * lax.rsqrt(ms + NORM_EPS) * g_ref[h:h + 1, :]
            gt = gate_ref[sl, cols(h)]
            yr_ref[sl, cols(h)] = (y * (gt * jax.nn.sigmoid(gt))).astype(BF16)

    for c in range(out_chunks):
        o_ref[:, ocols(c)] += jnp.dot(yr_ref[...], w_ref[:kr, ocols(c)], preferred_element_type=F32)

    wg_o[...] = wg_ref[...].astype(BF16)
    wu_o[...] = wu_ref[...].astype(BF16)
    wd_o[...] = wd_ref[...].astype(BF16)


def _ret_out(proj, gate, ret_g, log_decay, y_m, w_out, w_gate, w_up, w_down, layer, x2, B, S, t):
    T, D = x2.shape
    ts = t["ret_ts"]
    width = RET_HEADS * RET_DIM
    km = y_m.shape[1]
    assert width == GROUP and S % ts == 0 and ts % RET_CHUNK == 0
    tiles = S // ts
    tile = lambda last, group=0: pl.BlockSpec((ts, last), lambda b, s: (b * tiles + s, group))
    steps = B * tiles
    F = w_gate.shape[2]
    assert D % (steps * V7X_BF16_SUBLANES) == 0 and F % (steps * V7X_BF16_SUBLANES) == 0
    up_rows, down_rows = D // steps, F // steps
    w_slice = lambda rows, width_: pl.BlockSpec((None, rows, width_), lambda b, s: (layer, b * tiles + s, 0))
    w_cast = lambda rows, width_: pl.BlockSpec((rows, width_), lambda b, s: (b * tiles + s, 0))
    vmem = ((width + km) * D * 2 + 2 * ts * (3 * width * 2 + width * 4 + km * 2 + 2 * D * 4)
            + ts * width * 2 + RET_HEADS * RET_DIM * RET_DIM * 4
            + 2 * (2 * up_rows * F + down_rows * D) * (4 + 2))
    return pl.pallas_call(
        functools.partial(_ret_out_kernel, n_chunks=ts // RET_CHUNK),
        grid=(B, tiles),
        in_specs=[
            pl.BlockSpec(memory_space=pltpu.SMEM),
            tile(width, 0), tile(width, 1), tile(width, 2),
            tile(width),
            pl.BlockSpec((RET_HEADS, RET_DIM), lambda b, s: (0, 0)),
            tile(km),
            pl.BlockSpec((None, width + km, D), lambda b, s: (layer, 0, 0), pipeline_mode=pl.Buffered(1)),
            tile(D),
            w_slice(up_rows, F), w_slice(up_rows, F), w_slice(down_rows, D),
        ],
        out_specs=[tile(D), w_cast(up_rows, F), w_cast(up_rows, F), w_cast(down_rows, D)],
        out_shape=[jax.ShapeDtypeStruct((T, D), F32), jax.ShapeDtypeStruct((D, F), BF16),
                   jax.ShapeDtypeStruct((D, F), BF16), jax.ShapeDtypeStruct((F, D), BF16)],
        scratch_shapes=[pltpu.VMEM((RET_HEADS, RET_DIM, RET_DIM), F32), pltpu.VMEM((ts, width), BF16)],
        compiler_params=_params(("arbitrary", "arbitrary"), _vmem_request(vmem)),
    )(log_decay, proj, proj, proj, gate, ret_g, y_m, w_out, x2, w_gate, w_up, w_down)


def _moba_kernel(slope_ref, q_ref, k_ref, v_ref, o_ref, kmh_ref, kml_ref, vt_ref, kx_ref, qa_ref,
                 *, n_blocks, heads):
    hg = pl.program_id(1)
    step = pl.program_id(2)
    blocks_per_step = q_ref.shape[0] // MOBA_BLOCK
    L = MOBA_BLOCK
    dh = MOBA_DIM
    lanes = lambda g: slice(g * dh, (g + 1) * dh)
    CONST_ROWS = 8
    MASK_ROW0 = CONST_ROWS
    ONES_ROWS = vt_ref.shape[1] - dh

    @pl.when(step == 0)
    def _():
        lane = lax.broadcasted_iota(jnp.int32, (L, dh), 1)
        key = lax.broadcasted_iota(jnp.int32, (L, dh), 0)
        for n in range(n_blocks):
            kx = jnp.where(lane < 3, key, jnp.where(lane < 6, n * L, jnp.where(lane == MASK_ROW0 + n, 1, 0)))
            kx_ref[n * L:(n + 1) * L, :] = kx.astype(F32).astype(BF16)
        for g in range(heads):
            for n in range(n_blocks):
                rows = slice(n * L, (n + 1) * L)
                kb = k_ref[rows, lanes(g)].astype(F32)
                mean = jnp.sum(kb, axis=0, keepdims=True) * (1.0 / L)
                hi = mean.astype(BF16)
                kmh_ref[g, n:n + 1, :] = hi
                kml_ref[g, n:n + 1, :] = (mean - hi.astype(F32)).astype(BF16)
                vt_ref[g, :dh, rows] = v_ref[rows, lanes(g)].astype(F32).T.astype(BF16)
            vt_ref[g, dh:, :] = jnp.ones((ONES_ROWS, vt_ref.shape[2]), BF16)

    def query_block(i, rows):
        blk_id = lax.broadcasted_iota(jnp.int32, (n_blocks, L), 0)
        blk_f = blk_id.astype(F32)
        past = blk_id < i
        base = pl.multiple_of(i * L, L)
        part_row = lax.broadcasted_iota(jnp.int32, (CONST_ROWS, L), 0)
        causal = (lax.broadcasted_iota(jnp.int32, (L, L), 1) >= lax.broadcasted_iota(jnp.int32, (L, L), 0))

        qts = [q_ref[rows, lanes(g)].astype(F32).T.astype(BF16) for g in range(heads)]
        gates = [jnp.dot(kmh_ref[g], qts[g], preferred_element_type=F32)
                 + jnp.dot(kml_ref[g], qts[g], preferred_element_type=F32) for g in range(heads)]
        for g in range(heads):
            gate = jnp.where(past, gates[g], -jnp.inf)
            chosen = jnp.zeros((n_blocks, L), jnp.bool_)
            for _ in range(MOBA_TOPK):
                best = jnp.max(gate, axis=0, keepdims=True)
                first = jnp.min(jnp.where(gate == best, blk_f, float(n_blocks)), axis=0, keepdims=True)
                hit = blk_f == first
                chosen = jnp.logical_or(chosen, hit)
                gate = jnp.where(hit, -jnp.inf, gate)
            mask_rows = jnp.where(jnp.logical_and(chosen, past), 0.0, NEG_INF)
            c2 = jnp.full((CONST_ROWS, L), slope_ref[hg * heads + g] * LOG2E, F32)
            c_hi = c2.astype(BF16).astype(F32)
            c_mid = (c2 - c_hi).astype(BF16).astype(F32)
            c_lo = c2 - c_hi - c_mid
            sel3 = part_row % 3
            const_rows = jnp.where(part_row < 6, jnp.where(sel3 == 0, c_hi, jnp.where(sel3 == 1, c_mid, c_lo)), 0.0)
            extra = jnp.concatenate(
                [const_rows, mask_rows, jnp.zeros((dh - CONST_ROWS - n_blocks, L), F32)], axis=0).astype(BF16)
            qa_ref[g, :dh, :] = qts[g]
            qa_ref[g, dh:, :] = extra

        def scores(g, off, n_keys, kx):
            k_aug = jnp.concatenate([k_ref[pl.ds(off, n_keys), lanes(g)], kx], axis=1)
            return jnp.dot(k_aug, qa_ref[g], preferred_element_type=F32)

        def pair_scores(jj):
            off = pl.multiple_of(jj * (2 * L), 2 * L)
            kx = kx_ref[pl.ds(off, 2 * L), :]
            return [scores(g, off, 2 * L, kx) for g in range(heads)]

        kx_lane = lax.broadcasted_iota(jnp.int32, (L, dh), 1)
        kx_own = jnp.where(kx_lane < MASK_ROW0, kx_ref[pl.ds(base, L), :].astype(F32), 0.0).astype(BF16)
        own = [scores(g, base, L, kx_own) for g in range(heads)]
        first = pair_scores(0)
        init = []
        for g in range(heads):
            s = jnp.where(causal, own[g], NEG_INF)
            m0 = jnp.max(s, axis=0, keepdims=True)
            p = jnp.exp2(s - m0)
            acc0 = jnp.dot(vt_ref[g, :, pl.ds(base, L)], p.astype(BF16), preferred_element_type=F32)
            init += [m0, acc0]

        def fold_pair(jj, raw, carry):
            off = pl.multiple_of(jj * (2 * L), 2 * L)
            out = []
            for g in range(heads):
                m_prev, acc = carry[2 * g:2 * g + 2]
                s = raw[g]
                m_new = jnp.maximum(m_prev, jnp.max(s, axis=0, keepdims=True))
                alpha = jnp.exp2(m_prev - m_new)
                p = jnp.exp2(s - m_new)
                pv = jnp.dot(vt_ref[g, :, pl.ds(off, 2 * L)], p.astype(BF16), preferred_element_type=F32)
                out += [m_new, alpha * acc + pv]
            return tuple(out)

        def past_pair(jj, carry):
            return fold_pair(jj, pair_scores(jj), carry)

        fin = lax.fori_loop(1, (i + 1) // 2, past_pair, fold_pair(0, first, tuple(init)))
        for g in range(heads):
            acc = fin[2 * g + 1]
            o_ref[rows, lanes(g)] = (acc[:dh] / acc[dh:dh + 1]).T.astype(BF16)

    for r in range(blocks_per_step):
        query_block(step * blocks_per_step + r, slice(r * L, (r + 1) * L))


def _moba(proj, slopes, B, S, t):
    T = B * S
    nb = S // MOBA_BLOCK
    G = t["moba_heads_per_step"]
    R = t["moba_query_blocks_per_step"]
    per = GROUP // (G * MOBA_DIM)
    q_off, k_off, v_off = 3 * per, 4 * per, 5 * per
    return pl.pallas_call(
        functools.partial(_moba_kernel, n_blocks=nb, heads=G),
        grid=(B, MOBA_HEADS // G, nb // R),
        in_specs=[
            pl.BlockSpec(memory_space=pltpu.SMEM),
            pl.BlockSpec((R * MOBA_BLOCK, G * MOBA_DIM), lambda b, h, i: (b * (nb // R) + i, q_off + h)),
            pl.BlockSpec((S, G * MOBA_DIM), lambda b, h, i: (b, k_off + h)),
            pl.BlockSpec((S, G * MOBA_DIM), lambda b, h, i: (b, v_off + h)),
        ],
        out_specs=pl.BlockSpec((R * MOBA_BLOCK, G * MOBA_DIM), lambda b, h, i: (b * (nb // R) + i, h)),
        out_shape=jax.ShapeDtypeStruct((T, MOBA_HEADS * MOBA_DIM), BF16),
        scratch_shapes=[
            pltpu.VMEM((G, nb, MOBA_DIM), BF16),
            pltpu.VMEM((G, nb, MOBA_DIM), BF16),
            pltpu.VMEM((G, MOBA_DIM + V7X_BF16_SUBLANES, S), BF16),
            pltpu.VMEM((S, MOBA_DIM), BF16),
            pltpu.VMEM((G, 2 * MOBA_DIM, MOBA_BLOCK), BF16),
        ],
        compiler_params=_params(("arbitrary", "arbitrary", "arbitrary"), t["vmem_limit"]),
    )(slopes, proj, proj, proj)


HALO = V7X_BF16_SUBLANES


def _ffn_kernel(x_ref, xh_ref, g_ref, wg_ref, wu_ref, cw_ref, cb_ref, wd_ref, o_ref, h_ref, *, tm, tiles_per_seq):
    i = pl.program_id(0)
    f = pl.program_id(1)

    def norm(x):
        ms = jnp.mean(x * x, axis=-1, keepdims=True)
        return (x * lax.rsqrt(ms + NORM_EPS) * g_ref[...]).astype(BF16)

    @pl.when(f == 0)
    def _():
        x = x_ref[...]
        h_ref[HALO:, :] = norm(x)
        seq_start = (i % tiles_per_seq) == 0
        h_ref[:HALO, :] = jnp.where(seq_start, jnp.zeros((HALO, x.shape[1]), BF16), norm(xh_ref[...]))
        o_ref[...] = x

    a = jnp.dot(h_ref[...], wg_ref[...], preferred_element_type=F32)
    u = jnp.dot(h_ref[HALO:, :], wu_ref[...], preferred_element_type=F32)
    tf = wg_ref.shape[1]
    fcols = pl.ds(pl.multiple_of(f * tf, tf), tf)
    conv = cb_ref[:, fcols]
    for tap in range(CONV_WIDTH):
        lo = HALO - (CONV_WIDTH - 1) + tap
        conv = conv + cw_ref[tap:tap + 1, fcols] * a[lo:lo + tm, :]
    act = (conv * jax.nn.sigmoid(conv) * u).astype(BF16)
    o_ref[...] += jnp.dot(act, wd_ref[...], preferred_element_type=F32)


def _conv_ffn(x2, ln_g, w_gate, w_up, conv_w, conv_b, w_down, S, t):
    T, D = x2.shape
    F = w_gate.shape[1]
    tm, tf = t["ffn_tm"], t["ffn_tf"]
    halo_blocks = tm // HALO
    return pl.pallas_call(
        functools.partial(_ffn_kernel, tm=tm, tiles_per_seq=S // tm),
        grid=(T // tm, F // tf),
        in_specs=[
            pl.BlockSpec((tm, D), lambda i, f: (i, 0)),
            pl.BlockSpec((HALO, D), lambda i, f: (jnp.maximum(i * halo_blocks - 1, 0), 0)),
            pl.BlockSpec((1, D), lambda i, f: (0, 0)),
            pl.BlockSpec((D, tf), lambda i, f: (0, f)),
            pl.BlockSpec((D, tf), lambda i, f: (0, f)),
            pl.BlockSpec((CONV_WIDTH, F), lambda i, f: (0, 0)),
            pl.BlockSpec((1, F), lambda i, f: (0, 0)),
            pl.BlockSpec((tf, D), lambda i, f: (f, 0)),
        ],
        out_specs=pl.BlockSpec((tm, D), lambda i, f: (i, 0)),
        out_shape=jax.ShapeDtypeStruct((T, D), F32),
        scratch_shapes=[pltpu.VMEM((HALO + tm, D), BF16)],
        compiler_params=_params(("arbitrary", "arbitrary"), t["vmem_limit"]),
    )(x2, x2, ln_g.reshape(1, D), w_gate, w_up, conv_w, conv_b.reshape(1, F), w_down)


def kernel(x, ln1_g, w_in, ret_norm_g, q_norm_g, k_norm_g, w_out, ln2_g, w_gate, w_up, conv_w, conv_b, w_down):
    B, S, D = x.shape
    depth = w_in.shape[0]
    t = _tiles()
    assert S % RET_CHUNK == 0 and S % MOBA_BLOCK == 0 and S % t["ffn_tm"] == 0
    assert w_in.shape[2] == 7 * GROUP

    log_decay = jnp.log(1.0 - 2.0 ** (-5.0 - jnp.arange(RET_HEADS, dtype=F32)))
    slopes = 2.0 ** (-8.0 * (jnp.arange(MOBA_HEADS, dtype=F32) + 1.0) / MOBA_HEADS)

    w_in, w_out = w_in.astype(BF16), w_out.astype(BF16)
    x2 = x.reshape(B * S, D)
    for l in range(depth):
        proj, gate = _in_proj(x2, ln1_g[l], w_in, l, q_norm_g[l], k_norm_g[l], t)
        y_m = _moba(proj, slopes, B, S, t)
        x2, wg, wu, wd = _ret_out(proj, gate, ret_norm_g[l], log_decay, y_m, w_out, w_gate, w_up, w_down, l, x2, B, S, t)
        x2 = _conv_ffn(x2, ln2_g[l], wg, wu, conv_w[l], conv_b[l], wd, S, t)
    return x2.reshape(B, S, D)
```
